```python
import math
import jax, jax.numpy as jnp
from jax import lax
import numpy as np

D_MODEL = 2048
BATCH = 2
SEQ = 4096
DEPTH = 1
DEC_BATCH = 32
DEC_SEQ = 1
PAST_LEN = 8192
PAGE_SIZE = 128

HEAD_DIM = 128
N_HEADS = D_MODEL // (2 * HEAD_DIM)
KDIM = 2 * HEAD_DIM
VDIM = 2 * HEAD_DIM
ATTN_W = N_HEADS * VDIM
D_CONV = D_MODEL // 2
CONV_W = 3
D_FF = ((8 * D_MODEL // 3 + 255) // 256) * 256
Q_BLOCK = 128
RMS_EPS = 1e-6
SPLIT_SIZES = (D_CONV, D_CONV, D_CONV, N_HEADS * KDIM, N_HEADS * KDIM, ATTN_W, D_MODEL, D_MODEL)
N_IN = 3 * D_CONV + 2 * N_HEADS * KDIM + ATTN_W + 2 * D_MODEL

kernel_name = 'hybrid_shortconv_diffattn_convffn_step'


def rmsnorm(x, w):
    xf = x.astype(jnp.float32)
    y = xf * lax.rsqrt(jnp.mean(xf * xf, axis=-1, keepdims=True) + RMS_EPS) * w.astype(jnp.float32)
    return y.astype(x.dtype)


def causal_dwconv(z, prev, w):
    width = w.shape[0]
    t = z.shape[1]
    zp = jnp.concatenate([prev.astype(z.dtype), z], axis=1)
    out = zp[:, 0:t] * w[0]
    for j in range(1, width):
        out = out + zp[:, j:j + t] * w[j]
    return out, zp[:, -(width - 1):]


def alibi_slopes():
    return jnp.exp2(-8.0 * jnp.arange(1, N_HEADS + 1, dtype=jnp.float32) / N_HEADS)


def diff_attn_core(q, k, v, q_pos, k_pos, lam):
    b, tq = q.shape[:2]
    tk = k.shape[1]
    qf = q.astype(jnp.float32).reshape(b, tq, N_HEADS, 2, HEAD_DIM)
    kf = k.astype(jnp.float32).reshape(b, tk, N_HEADS, 2, HEAD_DIM)
    s = jnp.einsum('bqhcd,bkhcd->bchqk', qf, kf) * (HEAD_DIM ** -0.5)
    dist = q_pos[:, None] - k_pos[None, :]
    bias = -alibi_slopes()[:, None, None] * dist.astype(jnp.float32)[None]
    s = jnp.where(dist >= 0, s + bias, -jnp.inf)
    p = jax.nn.softmax(s, axis=-1)
    a = p[:, 0] - lam * p[:, 1]
    return jnp.einsum('bhqk,bkhv->bqhv', a, v.astype(jnp.float32))


def prompt_attend(q, k, v, lam):
    b, t = q.shape[:2]
    n_blocks = t // Q_BLOCK
    k_pos = jnp.arange(t, dtype=jnp.int32)

    def block(i):
        qb = lax.dynamic_slice_in_dim(q, i * Q_BLOCK, Q_BLOCK, axis=1)
        q_pos = i * Q_BLOCK + jnp.arange(Q_BLOCK, dtype=jnp.int32)
        return diff_attn_core(qb, k, v, q_pos, k_pos, lam)

    o = lax.map(block, jnp.arange(n_blocks, dtype=jnp.int32))
    return o.transpose(1, 0, 2, 3, 4).reshape(b, t, N_HEADS, VDIM)


def make_sample_attend(cache_k, cache_v, page_table, layer):
    n_pages = page_table.shape[1]
    past = n_pages * cache_k.shape[2]

    def attend(q, k, v, lam):
        t = q.shape[1]
        q_pos = past + jnp.arange(t, dtype=jnp.int32)
        k_pos = jnp.arange(past + t, dtype=jnp.int32)

        def one(args):
            qs, ks, vs, pt = args
            kp = cache_k[layer, pt].reshape(past, N_HEADS, KDIM)
            vp = cache_v[layer, pt].reshape(past, N_HEADS, VDIM)
            k_all = jnp.concatenate([kp, ks.astype(kp.dtype)], axis=0)
            v_all = jnp.concatenate([vp, vs.astype(vp.dtype)], axis=0)
            return diff_attn_core(qs[None], k_all[None], v_all[None], q_pos, k_pos, lam)[0]

        return lax.map(one, (q, k, v, page_table))

    return attend


def hybrid_layer(x, conv_prev, ffn_prev, attend, lam_init, p):
    b, t, _ = x.shape
    h = rmsnorm(x, p['pre_mix_w'])
    proj = h @ p['w_in']
    split_points = [int(s) for s in np.cumsum(SPLIT_SIZES)[:-1]]
    u, bg, cg, q, k, v, ga, gb = jnp.split(proj, split_points, axis=-1)
    zc, conv_state = causal_dwconv(cg * u, conv_prev, p['conv_w'])
    y_conv = (bg * zc) @ p['w_conv_out']
    q = q.reshape(b, t, N_HEADS, KDIM)
    k = k.reshape(b, t, N_HEADS, KDIM)
    v = v.reshape(b, t, N_HEADS, VDIM)
    f32 = jnp.float32
    lam = (jnp.exp(jnp.sum(p['lambda_q1'].astype(f32) * p['lambda_k1'].astype(f32)))
           - jnp.exp(jnp.sum(p['lambda_q2'].astype(f32) * p['lambda_k2'].astype(f32))) + lam_init)
    o = attend(q, k, v, lam)
    o = rmsnorm(o, p['subln_w']) * (1.0 - lam_init)
    y_attn = o.reshape(b, t, ATTN_W).astype(x.dtype) @ p['w_attn_out']
    merged = jax.nn.sigmoid(ga) * y_conv + jax.nn.sigmoid(gb) * y_attn
    x = x + rmsnorm(merged @ p['w_o'], p['post_mix_w'])
    h2 = rmsnorm(x, p['pre_ffn_w'])
    up, ffn_state = causal_dwconv(h2 @ p['w_up'], ffn_prev, p['ffn_conv_w'])
    gate, val = jnp.split(up, 2, axis=-1)
    x = x + rmsnorm((jax.nn.silu(gate) * val) @ p['w_down'], p['post_ffn_w'])
    return x, k, v, conv_state, ffn_state


def setup_inputs(seed: int = 0) -> dict:
    key = jax.random.key(seed)
    ks = jax.random.split(key, 24)
    f32 = jnp.float32
    n_pages = PAST_LEN // PAGE_SIZE
    n_pool = (5 * DEC_BATCH * n_pages) // 4
    nrm = lambda k, shape, scale: jax.random.normal(k, shape, f32) * scale
    gain = lambda k, n: 1.0 + 0.02 * jax.random.normal(k, (DEPTH, n), f32)
    page_table = jax.random.permutation(ks[0], n_pool)[:DEC_BATCH * n_pages].reshape(DEC_BATCH, n_pages).astype(jnp.int32)
    return {
        'x_prompt': nrm(ks[1], (BATCH, SEQ, D_MODEL), 1.0),
        'x_sample': nrm(ks[2], (DEC_BATCH, DEC_SEQ, D_MODEL), 1.0),
        'cache_k': nrm(ks[3], (DEPTH, n_pool, PAGE_SIZE, N_HEADS, KDIM), 1.0),
        'cache_v': nrm(ks[4], (DEPTH, n_pool, PAGE_SIZE, N_HEADS, VDIM), 1.0),
        'page_table': page_table,
        'state_conv': nrm(ks[5], (DEPTH, DEC_BATCH, CONV_W - 1, D_CONV), 1.0),
        'state_ffn_conv': nrm(ks[6], (DEPTH, DEC_BATCH, CONV_W - 1, 2 * D_FF), 1.0),
        'pre_mix_w': gain(ks[7], D_MODEL),
        'w_in': nrm(ks[8], (DEPTH, D_MODEL, N_IN), D_MODEL ** -0.5),
        'conv_w': nrm(ks[9], (DEPTH, CONV_W, D_CONV), CONV_W ** -0.5),
        'w_conv_out': nrm(ks[10], (DEPTH, D_CONV, D_MODEL), D_CONV ** -0.5),
        'lambda_q1': nrm(ks[11], (DEPTH, HEAD_DIM), 0.1),
        'lambda_k1': nrm(ks[12], (DEPTH, HEAD_DIM), 0.1),
        'lambda_q2': nrm(ks[13], (DEPTH, HEAD_DIM), 0.1),
        'lambda_k2': nrm(ks[14], (DEPTH, HEAD_DIM), 0.1),
        'subln_w': gain(ks[15], VDIM),
        'w_attn_out': nrm(ks[16], (DEPTH, ATTN_W, D_MODEL), ATTN_W ** -0.5),
        'w_o': nrm(ks[17], (DEPTH, D_MODEL, D_MODEL), D_MODEL ** -0.5),
        'post_mix_w': gain(ks[18], D_MODEL),
        'pre_ffn_w': gain(ks[19], D_MODEL),
        'w_up': nrm(ks[20], (DEPTH, D_MODEL, 2 * D_FF), D_MODEL ** -0.5),
        'ffn_conv_w': nrm(ks[21], (DEPTH, CONV_W, 2 * D_FF), CONV_W ** -0.5),
        'w_down': nrm(ks[22], (DEPTH, D_FF, D_MODEL), D_FF ** -0.5),
        'post_ffn_w': gain(ks[23], D_MODEL),
    }


def reference(x_prompt, x_sample, cache_k, cache_v, page_table, state_conv, state_ffn_conv,
              pre_mix_w, w_in, conv_w, w_conv_out, lambda_q1, lambda_k1, lambda_q2, lambda_k2,
              subln_w, w_attn_out, w_o, post_mix_w, pre_ffn_w, w_up, ffn_conv_w, w_down, post_ffn_w):
    yp, ys = x_prompt, x_sample
    kp_l, vp_l, cp_l, fp_l = [], [], [], []
    ks_l, vs_l, cs_l, fs_l = [], [], [], []
    for l in range(DEPTH):
        lam_init = 0.8 - 0.6 * math.exp(-0.3 * l)
        p = {
            'pre_mix_w': pre_mix_w[l], 'w_in': w_in[l], 'conv_w': conv_w[l], 'w_conv_out': w_conv_out[l],
            'lambda_q1': lambda_q1[l], 'lambda_k1': lambda_k1[l], 'lambda_q2': lambda_q2[l], 'lambda_k2': lambda_k2[l],
            'subln_w': subln_w[l], 'w_attn_out': w_attn_out[l], 'w_o': w_o[l], 'post_mix_w': post_mix_w[l],
            'pre_ffn_w': pre_ffn_w[l], 'w_up': w_up[l], 'ffn_conv_w': ffn_conv_w[l], 'w_down': w_down[l],
            'post_ffn_w': post_ffn_w[l],
        }
        b = yp.shape[0]
        conv0 = jnp.zeros((b, CONV_W - 1, D_CONV), yp.dtype)
        ffn0 = jnp.zeros((b, CONV_W - 1, 2 * D_FF), yp.dtype)
        yp, k_p, v_p, c_p, f_p = hybrid_layer(yp, conv0, ffn0, prompt_attend, lam_init, p)
        ys, k_s, v_s, c_s, f_s = hybrid_layer(ys, state_conv[l], state_ffn_conv[l],
                                              make_sample_attend(cache_k, cache_v, page_table, l), lam_init, p)
        kp_l.append(k_p); vp_l.append(v_p); cp_l.append(c_p); fp_l.append(f_p)
        ks_l.append(k_s); vs_l.append(v_s); cs_l.append(c_s); fs_l.append(f_s)
    return (yp, ys,
            jnp.stack(kp_l), jnp.stack(vp_l), jnp.stack(cp_l), jnp.stack(fp_l),
            jnp.stack(ks_l), jnp.stack(vs_l), jnp.stack(cs_l), jnp.stack(fs_l))
```

```python
import functools
import math

import jax
import jax.numpy as jnp
from jax import lax
from jax.experimental import pallas as pl
from jax.experimental.pallas import tpu as pltpu

F32 = jnp.float32
BF16 = jnp.bfloat16

HEAD_DIM = 128
CONV_W = 3
RMS_EPS = 1e-6
NEG_BIG = -1e30
LANES = 128
SUBLANES = 8
VMEM_LIMIT = 56 * 1024 * 1024


def _cparams(sem):
    return pltpu.CompilerParams(dimension_semantics=sem, vmem_limit_bytes=VMEM_LIMIT)


def _rms(x, w):
    ms = jnp.mean(x * x, axis=-1, keepdims=True)
    return x * lax.rsqrt(ms + RMS_EPS) * w


def _fill_normed(x_ref, nw_ref, h_ref, chunk):
    nw = nw_ref[...]

    def body(c, carry):
        r = pl.multiple_of(c * chunk, chunk)
        h_ref[pl.ds(r, chunk), :] = _rms(x_ref[pl.ds(r, chunk), :], nw).astype(BF16)
        return carry

    lax.fori_loop(0, x_ref.shape[0] // chunk, body, 0)


def _proj_kernel(x_ref, nw_ref, w_ref, *rest, outs, chunk):
    h_ref = rest[-1]

    @pl.when(pl.program_id(1) == 0)
    def _():
        _fill_normed(x_ref, nw_ref, h_ref, chunk)

    y = jnp.dot(h_ref[...], w_ref[...], preferred_element_type=F32)
    if outs == "f32":
        rest[0][...] = y
    elif outs == "bf16":
        rest[0][...] = y.astype(BF16)
    else:
        rest[0][...] = y
        rest[1][...] = y.astype(BF16)


def _proj(x, nw, w, col0, ncols, outs, tm, tn=512):
    rows, d = x.shape
    chunk = min(tm, 256)
    c0 = col0 // tn
    dts = {"f32": [F32], "bf16": [BF16], "both": [F32, BF16]}[outs]
    out = pl.pallas_call(
        functools.partial(_proj_kernel, outs=outs, chunk=chunk),
        grid=(rows // tm, ncols // tn),
        in_specs=[
            pl.BlockSpec((tm, d), lambda i, j: (i, 0)),
            pl.BlockSpec((1, d), lambda i, j: (0, 0)),
            pl.BlockSpec((d, tn), lambda i, j: (0, c0 + j)),
        ],
        out_specs=[pl.BlockSpec((tm, tn), lambda i, j: (i, j)) for _ in dts],
        out_shape=[jax.ShapeDtypeStruct((rows, ncols), dt) for dt in dts],
        scratch_shapes=[pltpu.VMEM((tm, d), BF16)],
        compiler_params=_cparams(("parallel", "arbitrary")),
    )(x, nw, w)
    return out


def _conv_proj_kernel(x_ref, nw_ref, wu_ref, wb_ref, wc_ref, z_ref, bg_ref, h_ref, *, chunk):
    @pl.when(pl.program_id(1) == 0)
    def _():
        _fill_normed(x_ref, nw_ref, h_ref, chunk)

    h = h_ref[...]
    u = jnp.dot(h, wu_ref[...], preferred_element_type=F32)
    cg = jnp.dot(h, wc_ref[...], preferred_element_type=F32)
    z_ref[...] = cg * u
    bg_ref[...] = jnp.dot(h, wb_ref[...], preferred_element_type=F32)


def _conv_proj(x, nw, w, d_conv, tm, tn=512):
    rows, d = x.shape
    chunk = min(tm, 256)
    nb = d_conv // tn
    return pl.pallas_call(
        functools.partial(_conv_proj_kernel, chunk=chunk),
        grid=(rows // tm, nb),
        in_specs=[
            pl.BlockSpec((tm, d), lambda i, j: (i, 0)),
            pl.BlockSpec((1, d), lambda i, j: (0, 0)),
            pl.BlockSpec((d, tn), lambda i, j: (0, j)),
            pl.BlockSpec((d, tn), lambda i, j: (0, nb + j)),
            pl.BlockSpec((d, tn), lambda i, j: (0, 2 * nb + j)),
        ],
        out_specs=[pl.BlockSpec((tm, tn), lambda i, j: (i, j))] * 2,
        out_shape=[jax.ShapeDtypeStruct((rows, d_conv), F32)] * 2,
        scratch_shapes=[pltpu.VMEM((tm, d), BF16)],
        compiler_params=_cparams(("parallel", "arbitrary")),
    )(x, nw, w, w, w)


def _lambda_value(lq1_ref, lk1_ref, lq2_ref, lk2_ref, lam_init):
    a = jnp.sum(lq1_ref[...] * lk1_ref[...], axis=-1, keepdims=True)
    b = jnp.sum(lq2_ref[...] * lk2_ref[...], axis=-1, keepdims=True)
    return jnp.exp(a) - jnp.exp(b) + lam_init


def _head_out(o, sub_w, lam_init):
    return _rms(o, sub_w) * (1.0 - lam_init)


def _flash_kernel(slope_ref, q_ref, k_ref, v_ref, lq1_ref, lk1_ref, lq2_ref, lk2_ref,
                  sub_ref, o_ref, acc1_ref, acc2_ref, *, tq, lam_init):
    h = pl.program_id(1)
    qi = pl.program_id(2)
    neg_slope = -slope_ref[h]
    scale = HEAD_DIM ** -0.5

    q = q_ref[...]
    q1 = q[:, :HEAD_DIM]
    q2 = q[:, HEAD_DIM:]
    row = lax.broadcasted_iota(jnp.int32, (tq, tq), 0)
    col = lax.broadcasted_iota(jnp.int32, (tq, tq), 1)
    dmat = row - col
    bias0 = dmat.astype(F32) * neg_slope

    acc1_ref[...] = jnp.zeros_like(acc1_ref)
    acc2_ref[...] = jnp.zeros_like(acc2_ref)

    def one_map(qc, kc, v, m, l, acc_ref, c, diag):
        s = lax.dot_general(qc, kc, (((1,), (1,)), ((), ())), preferred_element_type=F32)
        s = s * scale + bias0
        if diag:
            s = jnp.where(dmat >= 0, s, NEG_BIG)
        m_new = jnp.maximum(m, jnp.max(s, axis=-1, keepdims=True) + c)
        alpha = jnp.exp(m - m_new)
        p = jnp.exp(s - (m_new - c))
        l_new = alpha * l + jnp.sum(p, axis=-1, keepdims=True)
        acc_ref[...] = alpha * acc_ref[...] + jnp.dot(p.astype(BF16), v, preferred_element_type=F32)
        return m_new, l_new

    def step(ki, carry, diag):
        m1, l1, m2, l2 = carry
        off = pl.multiple_of(ki * tq, tq)
        k = k_ref[pl.ds(off, tq), :]
        v = v_ref[pl.ds(off, tq), :]
        c = neg_slope * ((qi - ki) * tq).astype(F32)
        m1, l1 = one_map(q1, k[:, :HEAD_DIM], v, m1, l1, acc1_ref, c, diag)
        m2, l2 = one_map(q2, k[:, HEAD_DIM:], v, m2, l2, acc2_ref, c, diag)
        return m1, l1, m2, l2

    m0 = jnp.full((tq, 1), NEG_BIG, F32)
    l0 = jnp.zeros((tq, 1), F32)
    carry = lax.fori_loop(0, qi, lambda ki, cr: step(ki, cr, False), (m0, l0, m0, l0))
    m1, l1, m2, l2 = step(qi, carry, True)

    lam = _lambda_value(lq1_ref, lk1_ref, lq2_ref, lk2_ref, lam_init)
    o = acc1_ref[...] / l1 - lam * (acc2_ref[...] / l2)
    o_ref[...] = _head_out(o, sub_ref[...], lam_init).astype(BF16)


def _flash_attention(q, k, v, slopes, lams, sub_w, batch, seq, n_heads, lam_init, tq=512):
    kd = 2 * HEAD_DIM
    nq = seq // tq
    lam_spec = pl.BlockSpec((1, HEAD_DIM), lambda b, h, i: (0, 0))
    return pl.pallas_call(
        functools.partial(_flash_kernel, tq=tq, lam_init=lam_init),
        grid=(batch, n_heads, nq),
        in_specs=[
            pl.BlockSpec(memory_space=pltpu.SMEM),
            pl.BlockSpec((tq, kd), lambda b, h, i: (b * nq + i, h)),
            pl.BlockSpec((seq, kd), lambda b, h, i: (b, h)),
            pl.BlockSpec((seq, kd), lambda b, h, i: (b, h)),
            lam_spec, lam_spec, lam_spec, lam_spec,
            pl.BlockSpec((1, kd), lambda b, h, i: (0, 0)),
        ],
        out_specs=pl.BlockSpec((tq, kd), lambda b, h, i: (b * nq + i, h)),
        out_shape=jax.ShapeDtypeStruct(q.shape, BF16),
        scratch_shapes=[pltpu.VMEM((tq, kd), F32), pltpu.VMEM((tq, kd), F32)],
        compiler_params=_cparams(("parallel", "parallel", "arbitrary")),
    )(slopes, q, k, v, *lams, sub_w)


def _decode_kernel(pt_ref, q_ref, kn_ref, vn_ref, lq1_ref, lk1_ref, lq2_ref, lk2_ref,
                   sub_ref, slope_ref, *rest, pages_per_step, page, n_pages, lam_init):
    del pt_ref
    k_refs = rest[:pages_per_step]
    v_refs = rest[pages_per_step:2 * pages_per_step]
    o_ref, m_ref, l_ref, acc_ref, s_ref = rest[2 * pages_per_step:]
    g = pl.program_id(1)
    past = n_pages * page

    @pl.when(g == 0)
    def _():
        m_ref[...] = jnp.full_like(m_ref, NEG_BIG)
        l_ref[...] = jnp.zeros_like(l_ref)
        acc_ref[...] = jnp.zeros_like(acc_ref)

    qv = q_ref[0] * (HEAD_DIM ** -0.5)
    q1 = qv[:, :HEAD_DIM]
    q2 = qv[:, HEAD_DIM:]
    slope = slope_ref[...]

    def rowsum(x):
        return jnp.broadcast_to(jnp.sum(x, axis=-1, keepdims=True), x.shape)

    def twice(p):
        return jnp.concatenate([p, p], axis=-1)

    def rescale(c, m_new):
        alpha = jnp.exp(m_ref[c] - m_new)
        m_ref[c] = m_new
        l_ref[c] = alpha * l_ref[c]
        acc_ref[c] = twice(alpha) * acc_ref[c]

    for r in range(pages_per_step):
        k_ref = k_refs[r]
        v_ref = v_refs[r]
        first_pos = (g * pages_per_step + r) * page
        bias_first = slope * (first_pos - past).astype(F32)

        def scores(t, carry):
            mx1, mx2, bias = carry
            kt = k_ref[t]
            s1 = rowsum(kt[:, :HEAD_DIM] * q1) + bias
            s2 = rowsum(kt[:, HEAD_DIM:] * q2) + bias
            s_ref[t, 0] = s1
            s_ref[t, 1] = s2
            return jnp.maximum(mx1, s1), jnp.maximum(mx2, s2), bias + slope

        mx1, mx2, _ = lax.fori_loop(0, page, scores, (m_ref[0], m_ref[1], bias_first), unroll=8)
        rescale(0, mx1)
        rescale(1, mx2)

        def accumulate(t, carry):
            l1, l2, a1, a2 = carry
            p1 = jnp.exp(s_ref[t, 0] - mx1)
            p2 = jnp.exp(s_ref[t, 1] - mx2)
            vt = v_ref[t]
            return l1 + p1, l2 + p2, a1 + twice(p1) * vt, a2 + twice(p2) * vt

        l1, l2, a1, a2 = lax.fori_loop(
            0, page, accumulate, (l_ref[0], l_ref[1], acc_ref[0], acc_ref[1]), unroll=8)
        l_ref[0] = l1
        l_ref[1] = l2
        acc_ref[0] = a1
        acc_ref[1] = a2

    @pl.when(g == pl.num_programs(1) - 1)
    def _():
        kn = kn_ref[0]
        vn = vn_ref[0]
        s1 = rowsum(kn[:, :HEAD_DIM] * q1)
        s2 = rowsum(kn[:, HEAD_DIM:] * q2)
        mn1 = jnp.maximum(m_ref[0], s1)
        mn2 = jnp.maximum(m_ref[1], s2)
        rescale(0, mn1)
        rescale(1, mn2)
        p1 = jnp.exp(s1 - mn1)
        p2 = jnp.exp(s2 - mn2)
        o1 = (acc_ref[0] + twice(p1) * vn) / twice(l_ref[0] + p1)
        o2 = (acc_ref[1] + twice(p2) * vn) / twice(l_ref[1] + p2)
        lam = _lambda_value(lq1_ref, lk1_ref, lq2_ref, lk2_ref, lam_init)
        o_ref[0] = _head_out(o1 - lam * o2, sub_ref[...], lam_init)


def _decode_attention(q, k_new, v_new, cache_k, cache_v, page_table, layer, slopes, lams,
                      sub_w, lam_init, pages_per_step=8):
    nb, n_heads, kd = q.shape
    n_pages = page_table.shape[1]
    page = cache_k.shape[2]
    steps = n_pages // pages_per_step
    slope_rep = jnp.broadcast_to(slopes[:, None], (n_heads, LANES))

    tok_spec = pl.BlockSpec((1, n_heads, kd), lambda b, g, pt: (b, 0, 0))
    lam_spec = pl.BlockSpec((1, HEAD_DIM), lambda b, g, pt: (0, 0))

    def page_spec(r):
        return pl.BlockSpec(
            (None, None, page, n_heads, kd),
            lambda b, g, pt: (layer, pt[b, g * pages_per_step + r], 0, 0, 0))

    page_specs = [page_spec(r) for r in range(pages_per_step)]
    grid_spec = pltpu.PrefetchScalarGridSpec(
        num_scalar_prefetch=1,
        grid=(nb, steps),
        in_specs=[tok_spec, tok_spec, tok_spec, lam_spec, lam_spec, lam_spec, lam_spec,
                  pl.BlockSpec((1, kd), lambda b, g, pt: (0, 0)),
                  pl.BlockSpec((n_heads, LANES), lambda b, g, pt: (0, 0))]
                 + page_specs + page_specs,
        out_specs=tok_spec,
        scratch_shapes=[
            pltpu.VMEM((2, n_heads, LANES), F32),
            pltpu.VMEM((2, n_heads, LANES), F32),
            pltpu.VMEM((2, n_heads, kd), F32),
            pltpu.VMEM((page, 2, n_heads, LANES), F32),
        ],
    )
    return pl.pallas_call(
        functools.partial(_decode_kernel, pages_per_step=pages_per_step, page=page,
                          n_pages=n_pages, lam_init=lam_init),
        grid_spec=grid_spec,
        out_shape=jax.ShapeDtypeStruct((nb, n_heads, kd), F32),
        compiler_params=_cparams(("parallel", "arbitrary")),
    )(page_table, q, k_new, v_new, *lams, sub_w, slope_rep,
      *([cache_k] * pages_per_step), *([cache_v] * pages_per_step))


def _merge_kernel(*refs, mode, tm, tiles_per_seq):
    if mode == "seq":
        (z_ref, halo_ref, bg_ref, o_ref, ga_ref, gb_ref, x_ref, cw_ref, wc_ref, wa_ref,
         wo_ref, pw_ref, out_ref, zbuf_ref) = refs
        i = pl.program_id(0)
        z = z_ref[...]
        starts_seq = i % tiles_per_seq == 0
        zbuf_ref[0:SUBLANES, :] = jnp.where(starts_seq, 0.0, halo_ref[...])
        zbuf_ref[SUBLANES:, :] = z
        z1 = zbuf_ref[SUBLANES - 1:SUBLANES - 1 + tm, :]
        z2 = zbuf_ref[SUBLANES - 2:SUBLANES - 2 + tm, :]
    else:
        (z_ref, st0_ref, st1_ref, bg_ref, o_ref, ga_ref, gb_ref, x_ref, cw_ref, wc_ref,
         wa_ref, wo_ref, pw_ref, out_ref) = refs
        z = z_ref[...]
        z2 = st0_ref[...]
        z1 = st1_ref[...]
    cw = cw_ref[...]
    zc = z2 * cw[0:1] + z1 * cw[1:2] + z * cw[2:3]
    y_conv = jnp.dot((bg_ref[...] * zc).astype(BF16), wc_ref[...], preferred_element_type=F32)
    y_attn = jnp.dot(o_ref[...], wa_ref[...], preferred_element_type=F32)
    merged = jax.nn.sigmoid(ga_ref[...]) * y_conv + jax.nn.sigmoid(gb_ref[...]) * y_attn
    mo = jnp.dot(merged.astype(BF16), wo_ref[...], preferred_element_type=F32)
    out_ref[...] = x_ref[...] + _rms(mo, pw_ref[...])


def _merge(z, prev, bg, o, gates, x, conv_w, wc, wa, wo, post_w, mode, tm, seq):
    rows, d = x.shape
    dc = z.shape[1]
    row = lambda w: pl.BlockSpec((tm, w), lambda i: (i, 0))
    const = lambda a: pl.BlockSpec(a.shape, lambda i: (0, 0), pipeline_mode=pl.Buffered(1))
    if mode == "seq":
        hb = tm // SUBLANES
        prev_specs = [pl.BlockSpec((SUBLANES, dc), lambda i: (jnp.maximum(i * hb - 1, 0), 0))]
        prev_args = [z]
        scratch = [pltpu.VMEM((tm + SUBLANES, dc), F32)]
        tiles_per_seq = seq // tm
    else:
        prev_specs = [row(dc), row(dc)]
        prev_args = list(prev)
        scratch = []
        tiles_per_seq = 1
    return pl.pallas_call(
        functools.partial(_merge_kernel, mode=mode, tm=tm, tiles_per_seq=tiles_per_seq),
        grid=(rows // tm,),
        in_specs=[row(dc)] + prev_specs + [
            row(dc), row(d),
            pl.BlockSpec((tm, d), lambda i: (i, 0)),
            pl.BlockSpec((tm, d), lambda i: (i, 1)),
            row(d), const(conv_w), const(wc), const(wa), const(wo), const(post_w)],
        out_specs=row(d),
        out_shape=jax.ShapeDtypeStruct((rows, d), F32),
        scratch_shapes=scratch,
        compiler_params=_cparams(("arbitrary",)),
    )(z, *prev_args, bg, o, gates, gates, x, conv_w, wc, wa, wo, post_w)


def _ffn_kernel(*refs, mode, tm, tiles_per_seq, chunk):
    if mode == "seq":
        (x_ref, nw_ref, wg_ref, wv_ref, cwg_ref, cwv_ref, wd_ref, pw_ref,
         out_ref, sg_ref, sv_ref, h_ref, acc_ref, cg_ref, cv_ref, ug_ref, uv_ref) = refs
    else:
        (x_ref, nw_ref, wg_ref, wv_ref, cwg_ref, cwv_ref, wd_ref, pw_ref,
         g0_ref, g1_ref, v0_ref, v1_ref, out_ref, sg_ref, sv_ref, h_ref, acc_ref) = refs
    i = pl.program_id(0)
    j = pl.program_id(1)

    @pl.when(j == 0)
    def _():
        _fill_normed(x_ref, nw_ref, h_ref, chunk)
        acc_ref[...] = jnp.zeros_like(acc_ref)

    h = h_ref[...]

    def seq_branch(w_ref, cw_ref, carry_ref, ubuf_ref, state_ref):
        up = jnp.dot(h, w_ref[...], preferred_element_type=F32)
        @pl.when(i % tiles_per_seq == 0)
        def _():
            ubuf_ref[0:SUBLANES, :] = jnp.zeros((SUBLANES, up.shape[1]), F32)

        @pl.when(i % tiles_per_seq != 0)
        def _():
            ubuf_ref[0:SUBLANES, :] = carry_ref[j]

        ubuf_ref[SUBLANES:, :] = up
        carry_ref[j] = up[tm - SUBLANES:, :]
        state_ref[...] = up[tm - (CONV_W - 1):, :]
        cw = cw_ref[...]
        u1 = ubuf_ref[SUBLANES - 1:SUBLANES - 1 + tm, :]
        u2 = ubuf_ref[SUBLANES - 2:SUBLANES - 2 + tm, :]
        return u2 * cw[0:1] + u1 * cw[1:2] + up * cw[2:3]

    def state_branch(w_ref, cw_ref, p0_ref, p1_ref, state_ref):
        up = jnp.dot(h, w_ref[...], preferred_element_type=F32)
        state_ref[...] = up
        cw = cw_ref[...]
        return p0_ref[...] * cw[0:1] + p1_ref[...] * cw[1:2] + up * cw[2:3]

    if mode == "seq":
        gate = seq_branch(wg_ref, cwg_ref, cg_ref, ug_ref, sg_ref)
        val = seq_branch(wv_ref, cwv_ref, cv_ref, uv_ref, sv_ref)
    else:
        gate = state_branch(wg_ref, cwg_ref, g0_ref, g1_ref, sg_ref)
        val = state_branch(wv_ref, cwv_ref, v0_ref, v1_ref, sv_ref)
    act = (gate * jax.nn.sigmoid(gate)) * val
    acc_ref[...] += jnp.dot(act.astype(BF16), wd_ref[...], preferred_element_type=F32)

    @pl.when(j == pl.num_programs(1) - 1)
    def _():
        out_ref[...] = x_ref[...] + _rms(acc_ref[...], pw_ref[...])


def _ffn(x, nw, w_up, conv_w, w_down, post_w, prev, mode, tm, seq, tf=512):
    rows, d = x.shape
    d_ff = w_down.shape[0]
    nj = d_ff // tf
    chunk = min(tm, 256)
    common = [
        pl.BlockSpec((tm, d), lambda i, j: (i, 0)),
        pl.BlockSpec((1, d), lambda i, j: (0, 0)),
        pl.BlockSpec((d, tf), lambda i, j: (0, j)),
        pl.BlockSpec((d, tf), lambda i, j: (0, nj + j)),
        pl.BlockSpec((CONV_W, tf), lambda i, j: (0, j)),
        pl.BlockSpec((CONV_W, tf), lambda i, j: (0, nj + j)),
        pl.BlockSpec((tf, d), lambda i, j: (j, 0)),
        pl.BlockSpec((1, d), lambda i, j: (0, 0)),
    ]
    args = [x, nw, w_up, w_up, conv_w, conv_w, w_down, post_w]
    scratch = [pltpu.VMEM((tm, d), BF16), pltpu.VMEM((tm, d), F32)]
    if mode == "seq":
        tiles_per_seq = seq // tm
        n_seq = rows // seq
        st_spec = pl.BlockSpec((None, CONV_W - 1, tf), lambda i, j: (i // tiles_per_seq, 0, j))
        st_shape = jax.ShapeDtypeStruct((n_seq, CONV_W - 1, d_ff), F32)
        in_specs = common
        scratch += [pltpu.VMEM((nj, SUBLANES, tf), F32), pltpu.VMEM((nj, SUBLANES, tf), F32),
                    pltpu.VMEM((tm + SUBLANES, tf), F32), pltpu.VMEM((tm + SUBLANES, tf), F32)]
    else:
        tiles_per_seq = 1
        st_spec = pl.BlockSpec((tm, tf), lambda i, j: (i, j))
        st_shape = jax.ShapeDtypeStruct((rows, d_ff), F32)
        pspec = pl.BlockSpec((tm, tf), lambda i, j: (i, j))
        in_specs = common + [pspec] * 4
        args += list(prev)
    return pl.pallas_call(
        functools.partial(_ffn_kernel, mode=mode, tm=tm, tiles_per_seq=tiles_per_seq, chunk=chunk),
        grid=(rows // tm, nj),
        in_specs=in_specs,
        out_specs=[pl.BlockSpec((tm, d), lambda i, j: (i, 0)), st_spec, st_spec],
        out_shape=[jax.ShapeDtypeStruct((rows, d), F32), st_shape, st_shape],
        scratch_shapes=scratch,
        compiler_params=_cparams(("arbitrary", "arbitrary")),
    )(*args)


def kernel(x_prompt, x_sample, cache_k, cache_v, page_table, state_conv, state_ffn_conv,
           pre_mix_w, w_in, conv_w, w_conv_out, lambda_q1, lambda_k1, lambda_q2, lambda_k2,
           subln_w, w_attn_out, w_o, post_mix_w, pre_ffn_w, w_up, ffn_conv_w, w_down, post_ffn_w):
    batch, seq, d = x_prompt.shape
    nb = x_sample.shape[0]
    depth = w_in.shape[0]
    n_heads = cache_k.shape[3]
    kd = cache_k.shape[4]
    d_conv = conv_w.shape[2]
    d_ff = w_down.shape[1]
    rows = batch * seq
    qkv = n_heads * kd
    c_q, c_k, c_v, c_g = 3 * d_conv, 3 * d_conv + qkv, 3 * d_conv + 2 * qkv, 3 * d_conv + 3 * qkv
    slopes = jnp.exp2(-8.0 * jnp.arange(1, n_heads + 1, dtype=F32) / n_heads)

    xp = x_prompt.reshape(rows, d)
    xs = x_sample.reshape(nb, d)
    outs = {n: [] for n in ("kp", "vp", "cp", "fp", "ks", "vs", "cs", "fs")}
    tm = 512

    for l in range(depth):
        lam_init = 0.8 - 0.6 * math.exp(-0.3 * l)
        w_in_b = w_in[l].astype(BF16)
        wc_b = w_conv_out[l].astype(BF16)
        wa_b = w_attn_out[l].astype(BF16)
        wo_b = w_o[l].astype(BF16)
        wu_b = w_up[l].astype(BF16)
        wd_b = w_down[l].astype(BF16)
        row2 = lambda a: a[l].reshape(1, -1)
        lams = [row2(lambda_q1), row2(lambda_k1), row2(lambda_q2), row2(lambda_k2)]
        pre_w, post_w = row2(pre_mix_w), row2(post_mix_w)
        pre_f, post_f, sub_w = row2(pre_ffn_w), row2(post_ffn_w), row2(subln_w)

        z, bg = _conv_proj(xp, pre_w, w_in_b, d_conv, tm)
        (q_b,) = _proj(xp, pre_w, w_in_b, c_q, qkv, "bf16", tm)
        k_f, k_b = _proj(xp, pre_w, w_in_b, c_k, qkv, "both", tm)
        v_f, v_b = _proj(xp, pre_w, w_in_b, c_v, qkv, "both", tm)
        (gates,) = _proj(xp, pre_w, w_in_b, c_g, 2 * d, "f32", tm)
        o = _flash_attention(q_b, k_b, v_b, slopes, lams, sub_w, batch, seq, n_heads, lam_init)
        xp = _merge(z, None, bg, o, gates, xp, conv_w[l], wc_b, wa_b, wo_b, post_w, "seq", 256, seq)
        xp, fg, fv = _ffn(xp, pre_f, wu_b, ffn_conv_w[l], wd_b, post_f, None, "seq", tm, seq)
        outs["kp"].append(k_f.reshape(batch, seq, n_heads, kd))
        outs["vp"].append(v_f.reshape(batch, seq, n_heads, kd))
        outs["cp"].append(z.reshape(batch, seq, d_conv)[:, seq - (CONV_W - 1):])
        outs["fp"].append(jnp.concatenate([fg, fv], axis=-1))

        zs, bgs = _conv_proj(xs, pre_w, w_in_b, d_conv, nb)
        (qkv_s,) = _proj(xs, pre_w, w_in_b, c_q, 3 * qkv, "f32", nb)
        (gates_s,) = _proj(xs, pre_w, w_in_b, c_g, 2 * d, "f32", nb)
        q_s = qkv_s[:, :qkv].reshape(nb, n_heads, kd)
        k_s = qkv_s[:, qkv:2 * qkv].reshape(nb, n_heads, kd)
        v_s = qkv_s[:, 2 * qkv:].reshape(nb, n_heads, kd)
        o_s = _decode_attention(q_s, k_s, v_s, cache_k, cache_v, page_table, l, slopes, lams,
                                sub_w, lam_init)
        sc = state_conv[l]
        xs = _merge(zs, (sc[:, 0], sc[:, 1]), bgs, o_s.reshape(nb, qkv).astype(BF16), gates_s, xs,
                    conv_w[l], wc_b, wa_b, wo_b, post_w, "state", nb, 1)
        sf = state_ffn_conv[l]
        prev = (sf[:, 0, :d_ff], sf[:, 1, :d_ff], sf[:, 0, d_ff:], sf[:, 1, d_ff:])
        xs, ug, uv = _ffn(xs, pre_f, wu_b, ffn_conv_w[l], wd_b, post_f, prev, "state", nb, 1)
        outs["ks"].append(k_s.reshape(nb, 1, n_heads, kd))
        outs["vs"].append(v_s.reshape(nb, 1, n_heads, kd))
        outs["cs"].append(jnp.stack([sc[:, 1], zs], axis=1))
        outs["fs"].append(jnp.stack([sf[:, 1], jnp.concatenate([ug, uv], axis=-1)], axis=1))

    st = lambda n: jnp.stack(outs[n])
    return (xp.reshape(batch, seq, d), xs.reshape(nb, 1, d),
            st("kp"), st("vp"), st("cp"), st("fp"),
            st("ks"), st("vs"), st("cs"), st("fs"))
```

```python
import functools
import math

import jax
import jax.numpy as jnp
from jax import lax
from jax.experimental import pallas as pl
from jax.experimental.pallas import tpu as pltpu

F32 = jnp.float32
BF16 = jnp.bfloat16

HEAD_DIM = 128
CONV_W = 3
RMS_EPS = 1e-6
NEG_BIG = -1e30
LOG2E = 1.4426950408889634
LANES = 128
SUBLANES = 8
VMEM_LIMIT = 56 * 1024 * 1024
PARTIAL_SUMS = 1


def _cparams(sem):
    return pltpu.CompilerParams(dimension_semantics=sem, vmem_limit_bytes=VMEM_LIMIT)


def _rms(x, w):
    ms = jnp.mean(x * x, axis=-1, keepdims=True)
    return x * lax.rsqrt(ms + RMS_EPS) * w


def _norm_kernel(x_ref, nw_ref, h_ref):
    h_ref[...] = _rms(x_ref[...], nw_ref[...]).astype(BF16)


def _norm(x, nw, tm):
    rows, d = x.shape
    return pl.pallas_call(
        _norm_kernel,
        grid=(rows // tm,),
        in_specs=[pl.BlockSpec((tm, d), lambda i: (i, 0)), pl.BlockSpec((1, d), lambda i: (0, 0))],
        out_specs=pl.BlockSpec((tm, d), lambda i: (i, 0)),
        out_shape=jax.ShapeDtypeStruct((rows, d), BF16),
        compiler_params=_cparams(("parallel",)),
        name="rmsnorm",
    )(x, nw)


def _proj_kernel(h_ref, w_ref, *o_refs, outs):
    y = jnp.dot(h_ref[...], w_ref[...], preferred_element_type=F32)
    if outs == "f32":
        o_refs[0][...] = y
    elif outs == "bf16":
        o_refs[0][...] = y.astype(BF16)
    else:
        o_refs[0][...] = y
        o_refs[1][...] = y.astype(BF16)


def _proj(h, w, col0, ncols, outs, tm, tn, name):
    rows, d = h.shape
    c0 = col0 // tn
    dts = {"f32": [F32], "bf16": [BF16], "both": [F32, BF16]}[outs]
    return pl.pallas_call(
        functools.partial(_proj_kernel, outs=outs),
        grid=(rows // tm, ncols // tn),
        in_specs=[
            pl.BlockSpec((tm, d), lambda i, j: (i, 0)),
            pl.BlockSpec((d, tn), lambda i, j: (0, c0 + j)),
        ],
        out_specs=[pl.BlockSpec((tm, tn), lambda i, j: (i, j)) for _ in dts],
        out_shape=[jax.ShapeDtypeStruct((rows, ncols), dt) for dt in dts],
        compiler_params=_cparams(("parallel", "parallel")),
        name=name,
    )(h, w)


def _conv_proj_kernel(h_ref, wu_ref, wb_ref, wc_ref, z_ref, bg_ref):
    h = h_ref[...]
    u = jnp.dot(h, wu_ref[...], preferred_element_type=F32)
    cg = jnp.dot(h, wc_ref[...], preferred_element_type=F32)
    z_ref[...] = cg * u
    bg_ref[...] = jnp.dot(h, wb_ref[...], preferred_element_type=F32)


def _conv_proj(h, w, d_conv, tm, tn, name):
    rows, d = h.shape
    nb = d_conv // tn
    return pl.pallas_call(
        _conv_proj_kernel,
        grid=(rows // tm, nb),
        in_specs=[
            pl.BlockSpec((tm, d), lambda i, j: (i, 0)),
            pl.BlockSpec((d, tn), lambda i, j: (0, j)),
            pl.BlockSpec((d, tn), lambda i, j: (0, nb + j)),
            pl.BlockSpec((d, tn), lambda i, j: (0, 2 * nb + j)),
        ],
        out_specs=[pl.BlockSpec((tm, tn), lambda i, j: (i, j))] * 2,
        out_shape=[jax.ShapeDtypeStruct((rows, d_conv), F32)] * 2,
        compiler_params=_cparams(("parallel", "parallel")),
        name=name,
    )(h, w, w, w)


def _lambda_value(lq1_ref, lk1_ref, lq2_ref, lk2_ref, lam_init):
    a = jnp.sum(lq1_ref[...] * lk1_ref[...], axis=-1, keepdims=True)
    b = jnp.sum(lq2_ref[...] * lk2_ref[...], axis=-1, keepdims=True)
    return jnp.exp(a) - jnp.exp(b) + lam_init


def _head_out(o, sub_w, lam_init):
    return _rms(o, sub_w) * (1.0 - lam_init)


def _flash_kernel(slope_ref, q_ref, k_ref, v_ref, lq1_ref, lk1_ref, lq2_ref, lk2_ref,
                  sub_ref, o_ref, acc1_ref, acc2_ref, *, tq, rs, lam_init):
    h = pl.program_id(1)
    qi = pl.program_id(2)
    neg_slope2 = -slope_ref[h] * LOG2E
    scale2 = HEAD_DIM ** -0.5 * LOG2E

    row = lax.broadcasted_iota(jnp.int32, (rs, tq), 0)
    col = lax.broadcasted_iota(jnp.int32, (rs, tq), 1)
    dmat0 = row - col
    bias00 = dmat0.astype(F32) * neg_slope2
    biases = {r0: bias00 + neg_slope2 * r0 for r0 in range(0, tq, rs)}

    acc1_ref[...] = jnp.zeros_like(acc1_ref)
    acc2_ref[...] = jnp.zeros_like(acc2_ref)

    def one_map(col0, k, v, m, l, acc_ref, c, diag):
        ms, ls = [], []
        for r0 in range(0, tq, rs):
            nk = r0 + rs if diag else tq
            qc = q_ref[r0:r0 + rs, col0:col0 + HEAD_DIM]
            kc = k[:nk, col0:col0 + HEAD_DIM]
            s = lax.dot_general(qc, kc, (((1,), (1,)), ((), ())), preferred_element_type=F32)
            s = s * scale2 + biases[r0][:, :nk]
            if diag:
                s = jnp.where(dmat0[:, :nk] + r0 >= 0, s, NEG_BIG)
            m_old = m[r0:r0 + rs]
            m_new = jnp.maximum(m_old, jnp.max(s, axis=-1, keepdims=True) + c)
            alpha = jnp.exp2(m_old - m_new)
            p = jnp.exp2(s - (m_new - c))
            ls.append(alpha * l[r0:r0 + rs] + jnp.sum(p, axis=-1, keepdims=True))
            ms.append(m_new)
            acc_ref[r0:r0 + rs, :] = alpha * acc_ref[r0:r0 + rs, :] + jnp.dot(
                p.astype(BF16), v[:nk], preferred_element_type=F32)
        return jnp.concatenate(ms, axis=0), jnp.concatenate(ls, axis=0)

    def step(ki, carry, diag):
        m1, l1, m2, l2 = carry
        off = pl.multiple_of(ki * tq, tq)
        k = k_ref[pl.ds(off, tq), :]
        v = v_ref[pl.ds(off, tq), :]
        c = neg_slope2 * ((qi - ki) * tq).astype(F32)
        m1, l1 = one_map(0, k, v, m1, l1, acc1_ref, c, diag)
        m2, l2 = one_map(HEAD_DIM, k, v, m2, l2, acc2_ref, c, diag)
        return m1, l1, m2, l2

    m0 = jnp.full((tq, 1), NEG_BIG, F32)
    l0 = jnp.zeros((tq, 1), F32)
    carry = lax.fori_loop(0, qi, lambda ki, cr: step(ki, cr, False), (m0, l0, m0, l0))
    m1, l1, m2, l2 = step(qi, carry, True)

    lam = _lambda_value(lq1_ref, lk1_ref, lq2_ref, lk2_ref, lam_init)
    o = acc1_ref[...] / l1 - lam * (acc2_ref[...] / l2)
    o_ref[...] = _head_out(o, sub_ref[...], lam_init).astype(BF16)


def _flash_attention(q, k, v, slopes, lams, sub_w, batch, seq, n_heads, lam_init, tq=512, rs=512):
    kd = 2 * HEAD_DIM
    nq = seq // tq
    lam_spec = pl.BlockSpec((1, HEAD_DIM), lambda b, h, i: (0, 0))
    return pl.pallas_call(
        functools.partial(_flash_kernel, tq=tq, rs=rs, lam_init=lam_init),
        grid=(batch, n_heads, nq),
        in_specs=[
            pl.BlockSpec(memory_space=pltpu.SMEM),
            pl.BlockSpec((tq, kd), lambda b, h, i: (b * nq + i, h)),
            pl.BlockSpec((seq, kd), lambda b, h, i: (b, h)),
            pl.BlockSpec((seq, kd), lambda b, h, i: (b, h)),
            lam_spec, lam_spec, lam_spec, lam_spec,
            pl.BlockSpec((1, kd), lambda b, h, i: (0, 0)),
        ],
        out_specs=pl.BlockSpec((tq, kd), lambda b, h, i: (b * nq + i, h)),
        out_shape=jax.ShapeDtypeStruct(q.shape, BF16),
        scratch_shapes=[pltpu.VMEM((tq, kd), F32), pltpu.VMEM((tq, kd), F32)],
        compiler_params=_cparams(("parallel", "parallel", "arbitrary")),
        name="prompt_attention",
    )(slopes, q, k, v, *lams, sub_w)


def _decode_kernel(pt_ref, q_ref, kn_ref, vn_ref, lq1_ref, lk1_ref, lq2_ref, lk2_ref,
                   sub_ref, slope_ref, tbl_ref, *rest, pages_per_step, page, n_pages,
                   lam_init, tokens_per_iter):
    del pt_ref
    k_refs = rest[:pages_per_step]
    v_refs = rest[pages_per_step:2 * pages_per_step]
    o_ref, m_ref, l_ref, acc_ref, s_even_ref, s_odd_ref = rest[2 * pages_per_step:]
    s_refs = (s_even_ref, s_odd_ref)
    g = pl.program_id(1)
    past = n_pages * page
    n_iter = page // tokens_per_iter

    @pl.when(g == 0)
    def _():
        m_ref[...] = jnp.full_like(m_ref, NEG_BIG)
        l_ref[...] = jnp.zeros_like(l_ref)
        acc_ref[...] = jnp.zeros_like(acc_ref)

    qv = q_ref[0] * (HEAD_DIM ** -0.5 * LOG2E)
    q1 = qv[:, :HEAD_DIM]
    q2 = qv[:, HEAD_DIM:]
    slope2 = slope_ref[...]

    def rowsum(x):
        return jnp.broadcast_to(jnp.sum(x, axis=-1, keepdims=True), x.shape)

    def twice(p):
        return jnp.concatenate([p, p], axis=-1)

    def run(score_page, acc_page, shifts):
        if score_page is not None:
            k_ref, sa_ref = k_refs[score_page], s_refs[score_page % 2]
        if acc_page is not None:
            v_ref, sb_ref = v_refs[acc_page], s_refs[acc_page % 2]

        def body(it, carry):
            mx1, mx2, parts = carry
            parts = [list(p) for p in parts]
            base = it * tokens_per_iter
            for tt in range(tokens_per_iter):
                t = base + tt
                if score_page is not None:
                    kt = k_ref[t]
                    tb = tbl_ref[t]
                    u1 = rowsum(kt[:, :HEAD_DIM] * q1) + tb
                    u2 = rowsum(kt[:, HEAD_DIM:] * q2) + tb
                    sa_ref[t, 0] = u1
                    sa_ref[t, 1] = u2
                    mx1 = jnp.maximum(mx1, u1)
                    mx2 = jnp.maximum(mx2, u2)
                if acc_page is not None:
                    p1 = jnp.exp2(sb_ref[t, 0] - shifts[0])
                    p2 = jnp.exp2(sb_ref[t, 1] - shifts[1])
                    vt = v_ref[t]
                    part = parts[tt % PARTIAL_SUMS]
                    part[0] = part[0] + p1
                    part[1] = part[1] + p2
                    part[2] = part[2] + twice(p1) * vt
                    part[3] = part[3] + twice(p2) * vt
            return mx1, mx2, tuple(tuple(p) for p in parts)

        neg = jnp.full((SUBLANES, LANES), NEG_BIG, F32)
        zero = (jnp.zeros_like(l_ref[0]), jnp.zeros_like(l_ref[0]),
                jnp.zeros_like(acc_ref[0]), jnp.zeros_like(acc_ref[0]))
        first = (l_ref[0], l_ref[1], acc_ref[0], acc_ref[1])
        if acc_page is None:
            parts0 = ()
        else:
            parts0 = (first,) + (zero,) * (PARTIAL_SUMS - 1)
        mx1, mx2, parts = lax.fori_loop(0, n_iter, body, (neg, neg, parts0))
        if acc_page is not None:
            l_ref[0] = sum(p[0] for p in parts[1:]) + parts[0][0]
            l_ref[1] = sum(p[1] for p in parts[1:]) + parts[0][1]
            acc_ref[0] = sum(p[2] for p in parts[1:]) + parts[0][2]
            acc_ref[1] = sum(p[3] for p in parts[1:]) + parts[0][3]
        return mx1, mx2

    def new_max(c, m_cand):
        m_new = jnp.maximum(m_ref[c], m_cand)
        alpha = jnp.exp2(m_ref[c] - m_new)
        m_ref[c] = m_new
        l_ref[c] = alpha * l_ref[c]
        acc_ref[c] = twice(alpha) * acc_ref[c]
        return m_new

    def page_shifts(r, mx1, mx2):
        first_pos = (g * pages_per_step + r) * page
        c = slope2 * (first_pos - past).astype(F32)
        return new_max(0, mx1 + c) - c, new_max(1, mx2 + c) - c

    mx = run(0, None, None)
    for r in range(pages_per_step):
        shifts = page_shifts(r, *mx)
        if r + 1 < pages_per_step:
            mx = run(r + 1, r, shifts)
        else:
            run(None, r, shifts)

    @pl.when(g == pl.num_programs(1) - 1)
    def _():
        kn = kn_ref[0]
        vn = vn_ref[0]
        s1 = rowsum(kn[:, :HEAD_DIM] * q1)
        s2 = rowsum(kn[:, HEAD_DIM:] * q2)
        p1 = jnp.exp2(s1 - new_max(0, s1))
        p2 = jnp.exp2(s2 - new_max(1, s2))
        o1 = (acc_ref[0] + twice(p1) * vn) / twice(l_ref[0] + p1)
        o2 = (acc_ref[1] + twice(p2) * vn) / twice(l_ref[1] + p2)
        lam = _lambda_value(lq1_ref, lk1_ref, lq2_ref, lk2_ref, lam_init)
        o_ref[0] = _head_out(o1 - lam * o2, sub_ref[...], lam_init)


def _decode_attention(q, k_new, v_new, cache_k, cache_v, page_table, layer, slopes, lams,
                      sub_w, lam_init, pages_per_step=8, tokens_per_iter=128):
    nb, n_heads, kd = q.shape
    n_pages = page_table.shape[1]
    page = cache_k.shape[2]
    steps = n_pages // pages_per_step
    slope2 = jnp.broadcast_to((slopes * LOG2E)[:, None], (n_heads, LANES))
    tok_bias = jnp.arange(page, dtype=F32)[:, None, None] * slope2[None]

    tok_spec = pl.BlockSpec((1, n_heads, kd), lambda b, g, pt: (b, 0, 0))
    lam_spec = pl.BlockSpec((1, HEAD_DIM), lambda b, g, pt: (0, 0))

    def page_spec(r):
        return pl.BlockSpec(
            (None, None, page, n_heads, kd),
            lambda b, g, pt: (layer, pt[b, g * pages_per_step + r], 0, 0, 0))

    page_specs = [page_spec(r) for r in range(pages_per_step)]
    grid_spec = pltpu.PrefetchScalarGridSpec(
        num_scalar_prefetch=1,
        grid=(nb, steps),
        in_specs=[tok_spec, tok_spec, tok_spec, lam_spec, lam_spec, lam_spec, lam_spec,
                  pl.BlockSpec((1, kd), lambda b, g, pt: (0, 0)),
                  pl.BlockSpec((n_heads, LANES), lambda b, g, pt: (0, 0)),
                  pl.BlockSpec((page, n_heads, LANES), lambda b, g, pt: (0, 0, 0))]
                 + page_specs + page_specs,
        out_specs=tok_spec,
        scratch_shapes=[
            pltpu.VMEM((2, n_heads, LANES), F32),
            pltpu.VMEM((2, n_heads, LANES), F32),
            pltpu.VMEM((2, n_heads, kd), F32),
            pltpu.VMEM((page, 2, n_heads, LANES), F32),
            pltpu.VMEM((page, 2, n_heads, LANES), F32),
        ],
    )
    return pl.pallas_call(
        functools.partial(_decode_kernel, pages_per_step=pages_per_step, page=page,
                          n_pages=n_pages, lam_init=lam_init, tokens_per_iter=tokens_per_iter),
        grid_spec=grid_spec,
        out_shape=jax.ShapeDtypeStruct((nb, n_heads, kd), F32),
        compiler_params=_cparams(("parallel", "arbitrary")),
        name="sample_attention",
    )(page_table, q, k_new, v_new, *lams, sub_w, slope2, tok_bias,
      *([cache_k] * pages_per_step), *([cache_v] * pages_per_step))


def _merge_kernel(*refs, mode, tm, tiles_per_seq):
    if mode == "seq":
        (z_ref, halo_ref, bg_ref, o_ref, ga_ref, gb_ref, x_ref, cw_ref, wc_ref, wa_ref,
         wo_ref, pw_ref, fw_ref, out_ref, h_ref, zbuf_ref) = refs
        i = pl.program_id(0)
        z = z_ref[...]
        starts_seq = i % tiles_per_seq == 0
        zbuf_ref[0:SUBLANES, :] = jnp.where(starts_seq, 0.0, halo_ref[...])
        zbuf_ref[SUBLANES:, :] = z
        z1 = zbuf_ref[SUBLANES - 1:SUBLANES - 1 + tm, :]
        z2 = zbuf_ref[SUBLANES - 2:SUBLANES - 2 + tm, :]
    else:
        (z_ref, st0_ref, st1_ref, bg_ref, o_ref, ga_ref, gb_ref, x_ref, cw_ref, wc_ref,
         wa_ref, wo_ref, pw_ref, fw_ref, out_ref, h_ref) = refs
        z = z_ref[...]
        z2 = st0_ref[...]
        z1 = st1_ref[...]
    cw = cw_ref[...]
    zc = z2 * cw[0:1] + z1 * cw[1:2] + z * cw[2:3]
    y_conv = jnp.dot((bg_ref[...] * zc).astype(BF16), wc_ref[...], preferred_element_type=F32)
    y_attn = jnp.dot(o_ref[...], wa_ref[...], preferred_element_type=F32)
    merged = jax.nn.sigmoid(ga_ref[...]) * y_conv + jax.nn.sigmoid(gb_ref[...]) * y_attn
    mo = jnp.dot(merged.astype(BF16), wo_ref[...], preferred_element_type=F32)
    x_new = x_ref[...] + _rms(mo, pw_ref[...])
    out_ref[...] = x_new
    h_ref[...] = _rms(x_new, fw_ref[...]).astype(BF16)


def _merge(z, prev, bg, o, gates, x, conv_w, wc, wa, wo, post_w, ffn_w, mode, tm, seq, name):
    rows, d = x.shape
    dc = z.shape[1]
    row = lambda w: pl.BlockSpec((tm, w), lambda i: (i, 0))
    const = lambda a: pl.BlockSpec(a.shape, lambda i: (0, 0), pipeline_mode=pl.Buffered(1))
    if mode == "seq":
        hb = tm // SUBLANES
        prev_specs = [pl.BlockSpec((SUBLANES, dc), lambda i: (jnp.maximum(i * hb - 1, 0), 0))]
        prev_args = [z]
        scratch = [pltpu.VMEM((tm + SUBLANES, dc), F32)]
        tiles_per_seq = seq // tm
    else:
        prev_specs = [row(dc), row(dc)]
        prev_args = list(prev)
        scratch = []
        tiles_per_seq = 1
    return pl.pallas_call(
        functools.partial(_merge_kernel, mode=mode, tm=tm, tiles_per_seq=tiles_per_seq),
        grid=(rows // tm,),
        in_specs=[row(dc)] + prev_specs + [
            row(dc), row(d),
            pl.BlockSpec((tm, d), lambda i: (i, 0)),
            pl.BlockSpec((tm, d), lambda i: (i, 1)),
            row(d), const(conv_w), const(wc), const(wa), const(wo), const(post_w), const(ffn_w)],
        out_specs=[row(d), row(d)],
        out_shape=[jax.ShapeDtypeStruct((rows, d), F32), jax.ShapeDtypeStruct((rows, d), BF16)],
        scratch_shapes=scratch,
        compiler_params=_cparams(("arbitrary",)),
        name=name,
    )(z, *prev_args, bg, o, gates, gates, x, conv_w, wc, wa, wo, post_w, ffn_w)


def _ffn_kernel(*refs, mode, tm, tiles_per_seq):
    if mode == "seq":
        (x_ref, h_ref, wg_ref, wv_ref, cwg_ref, cwv_ref, wd_ref, pw_ref,
         out_ref, sg_ref, sv_ref, acc_ref, cg_ref, cv_ref, ug_ref, uv_ref) = refs
    else:
        (x_ref, h_ref, wg_ref, wv_ref, cwg_ref, cwv_ref, wd_ref, pw_ref,
         g0_ref, g1_ref, v0_ref, v1_ref, out_ref, sg_ref, sv_ref, acc_ref) = refs
    i = pl.program_id(0)
    j = pl.program_id(1)

    @pl.when(j == 0)
    def _():
        acc_ref[...] = jnp.zeros_like(acc_ref)

    h = h_ref[...]

    def seq_branch(w_ref, cw_ref, carry_ref, ubuf_ref, state_ref):
        up = jnp.dot(h, w_ref[...], preferred_element_type=F32)

        @pl.when(i % tiles_per_seq == 0)
        def _():
            ubuf_ref[0:SUBLANES, :] = jnp.zeros((SUBLANES, up.shape[1]), F32)

        @pl.when(i % tiles_per_seq != 0)
        def _():
            ubuf_ref[0:SUBLANES, :] = carry_ref[j]

        ubuf_ref[SUBLANES:, :] = up
        carry_ref[j] = up[tm - SUBLANES:, :]
        state_ref[...] = up[tm - (CONV_W - 1):, :]
        cw = cw_ref[...]
        u1 = ubuf_ref[SUBLANES - 1:SUBLANES - 1 + tm, :]
        u2 = ubuf_ref[SUBLANES - 2:SUBLANES - 2 + tm, :]
        return u2 * cw[0:1] + u1 * cw[1:2] + up * cw[2:3]

    def state_branch(w_ref, cw_ref, p0_ref, p1_ref, state_ref):
        up = jnp.dot(h, w_ref[...], preferred_element_type=F32)
        state_ref[...] = up
        cw = cw_ref[...]
        return p0_ref[...] * cw[0:1] + p1_ref[...] * cw[1:2] + up * cw[2:3]

    if mode == "seq":
        gate = seq_branch(wg_ref, cwg_ref, cg_ref, ug_ref, sg_ref)
        val = seq_branch(wv_ref, cwv_ref, cv_ref, uv_ref, sv_ref)
    else:
        gate = state_branch(wg_ref, cwg_ref, g0_ref, g1_ref, sg_ref)
        val = state_branch(wv_ref, cwv_ref, v0_ref, v1_ref, sv_ref)
    act = (gate * jax.nn.sigmoid(gate)) * val
    acc_ref[...] += jnp.dot(act.astype(BF16), wd_ref[...], preferred_element_type=F32)

    @pl.when(j == pl.num_programs(1) - 1)
    def _():
        out_ref[...] = x_ref[...] + _rms(acc_ref[...], pw_ref[...])


def _ffn(x, h, w_up, conv_w, w_down, post_w, prev, mode, tm, seq, tf, name):
    rows, d = x.shape
    d_ff = w_down.shape[0]
    nj = d_ff // tf
    common = [
        pl.BlockSpec((tm, d), lambda i, j: (i, 0)),
        pl.BlockSpec((tm, d), lambda i, j: (i, 0)),
        pl.BlockSpec((d, tf), lambda i, j: (0, j)),
        pl.BlockSpec((d, tf), lambda i, j: (0, nj + j)),
        pl.BlockSpec((CONV_W, tf), lambda i, j: (0, j)),
        pl.BlockSpec((CONV_W, tf), lambda i, j: (0, nj + j)),
        pl.BlockSpec((tf, d), lambda i, j: (j, 0)),
        pl.BlockSpec((1, d), lambda i, j: (0, 0)),
    ]
    args = [x, h, w_up, w_up, conv_w, conv_w, w_down, post_w]
    scratch = [pltpu.VMEM((tm, d), F32)]
    if mode == "seq":
        tiles_per_seq = seq // tm
        st_spec = pl.BlockSpec((None, CONV_W - 1, tf), lambda i, j: (i, 0, j))
        st_shape = jax.ShapeDtypeStruct((rows // tm, CONV_W - 1, d_ff), F32)
        in_specs = common
        scratch += [pltpu.VMEM((nj, SUBLANES, tf), F32), pltpu.VMEM((nj, SUBLANES, tf), F32),
                    pltpu.VMEM((tm + SUBLANES, tf), F32), pltpu.VMEM((tm + SUBLANES, tf), F32)]
    else:
        tiles_per_seq = 1
        st_spec = pl.BlockSpec((tm, tf), lambda i, j: (i, j))
        st_shape = jax.ShapeDtypeStruct((rows, d_ff), F32)
        pspec = pl.BlockSpec((tm, tf), lambda i, j: (i, j))
        in_specs = common + [pspec] * 4
        args += list(prev)
    return pl.pallas_call(
        functools.partial(_ffn_kernel, mode=mode, tm=tm, tiles_per_seq=tiles_per_seq),
        grid=(rows // tm, nj),
        in_specs=in_specs,
        out_specs=[pl.BlockSpec((tm, d), lambda i, j: (i, 0)), st_spec, st_spec],
        out_shape=[jax.ShapeDtypeStruct((rows, d), F32), st_shape, st_shape],
        scratch_shapes=scratch,
        compiler_params=_cparams(("arbitrary", "arbitrary")),
        name=name,
    )(*args)


def kernel(x_prompt, x_sample, cache_k, cache_v, page_table, state_conv, state_ffn_conv,
           pre_mix_w, w_in, conv_w, w_conv_out, lambda_q1, lambda_k1, lambda_q2, lambda_k2,
           subln_w, w_attn_out, w_o, post_mix_w, pre_ffn_w, w_up, ffn_conv_w, w_down, post_ffn_w):
    batch, seq, d = x_prompt.shape
    nb = x_sample.shape[0]
    depth = w_in.shape[0]
    n_heads = cache_k.shape[3]
    kd = cache_k.shape[4]
    d_conv = conv_w.shape[2]
    d_ff = w_down.shape[1]
    rows = batch * seq
    qkv = n_heads * kd
    c_q, c_k, c_v, c_g = 3 * d_conv, 3 * d_conv + qkv, 3 * d_conv + 2 * qkv, 3 * d_conv + 3 * qkv
    slopes = jnp.exp2(-8.0 * jnp.arange(1, n_heads + 1, dtype=F32) / n_heads)

    xp = x_prompt.reshape(rows, d)
    xs = x_sample.reshape(nb, d)
    outs = {n: [] for n in ("kp", "vp", "cp", "fp", "ks", "vs", "cs", "fs")}
    tm, tn = 1024, 1024
    tm_merge, tm_ffn, tf = 256, 512, 512

    for l in range(depth):
        lam_init = 0.8 - 0.6 * math.exp(-0.3 * l)
        w_in_b = w_in[l].astype(BF16)
        wc_b = w_conv_out[l].astype(BF16)
        wa_b = w_attn_out[l].astype(BF16)
        wo_b = w_o[l].astype(BF16)
        wu_b = w_up[l].astype(BF16)
        wd_b = w_down[l].astype(BF16)
        row2 = lambda a: a[l].reshape(1, -1)
        lams = [row2(lambda_q1), row2(lambda_k1), row2(lambda_q2), row2(lambda_k2)]
        pre_w, post_w = row2(pre_mix_w), row2(post_mix_w)
        pre_f, post_f, sub_w = row2(pre_ffn_w), row2(post_ffn_w), row2(subln_w)

        hp = _norm(xp, pre_w, 512)
        z, bg = _conv_proj(hp, w_in_b, d_conv, tm, 512, "p_conv_proj")
        (q_b,) = _proj(hp, w_in_b, c_q, qkv, "bf16", tm, tn, "p_q_proj")
        k_f, k_b = _proj(hp, w_in_b, c_k, qkv, "both", tm, tn, "p_k_proj")
        v_f, v_b = _proj(hp, w_in_b, c_v, qkv, "both", tm, tn, "p_v_proj")
        (gates,) = _proj(hp, w_in_b, c_g, 2 * d, "f32", tm, tn, "p_gate_proj")
        o = _flash_attention(q_b, k_b, v_b, slopes, lams, sub_w, batch, seq, n_heads, lam_init)
        xp, hf = _merge(z, None, bg, o, gates, xp, conv_w[l], wc_b, wa_b, wo_b, post_w, pre_f,
                        "seq", tm_merge, seq, "p_merge")
        xp, fg, fv = _ffn(xp, hf, wu_b, ffn_conv_w[l], wd_b, post_f, None, "seq", tm_ffn, seq, tf,
                          "p_ffn")
        last = seq // tm_ffn - 1
        outs["kp"].append(k_f.reshape(batch, seq, n_heads, kd))
        outs["vp"].append(v_f.reshape(batch, seq, n_heads, kd))
        outs["cp"].append(z.reshape(batch, seq, d_conv)[:, seq - (CONV_W - 1):])
        outs["fp"].append(jnp.concatenate([fg, fv], axis=-1)[last::seq // tm_ffn])

        hs = _norm(xs, pre_w, nb)
        zs, bgs = _conv_proj(hs, w_in_b, d_conv, nb, 512, "s_conv_proj")
        (qkv_s,) = _proj(hs, w_in_b, c_q, 3 * qkv, "f32", nb, tn, "s_qkv_proj")
        (gates_s,) = _proj(hs, w_in_b, c_g, 2 * d, "f32", nb, tn, "s_gate_proj")
        q_s = qkv_s[:, :qkv].reshape(nb, n_heads, kd)
        k_s = qkv_s[:, qkv:2 * qkv].reshape(nb, n_heads, kd)
        v_s = qkv_s[:, 2 * qkv:].reshape(nb, n_heads, kd)
        o_s = _decode_attention(q_s, k_s, v_s, cache_k, cache_v, page_table, l, slopes, lams,
                                sub_w, lam_init)
        sc = state_conv[l]
        xs, hfs = _merge(zs, (sc[:, 0], sc[:, 1]), bgs, o_s.reshape(nb, qkv).astype(BF16), gates_s,
                         xs, conv_w[l], wc_b, wa_b, wo_b, post_w, pre_f, "state", nb, 1, "s_merge")
        sf = state_ffn_conv[l]
        prev = (sf[:, 0, :d_ff], sf[:, 1, :d_ff], sf[:, 0, d_ff:], sf[:, 1, d_ff:])
        xs, ug, uv = _ffn(xs, hfs, wu_b, ffn_conv_w[l], wd_b, post_f, prev, "state", nb, 1, tf,
                          "s_ffn")
        outs["ks"].append(k_s.reshape(nb, 1, n_heads, kd))
        outs["vs"].append(v_s.reshape(nb, 1, n_heads, kd))
        outs["cs"].append(jnp.stack([sc[:, 1], zs], axis=1))
        outs["fs"].append(jnp.stack([sf[:, 1], jnp.concatenate([ug, uv], axis=-1)], axis=1))

    st = lambda n: jnp.stack(outs[n])
    return (xp.reshape(batch, seq, d), xs.reshape(nb, 1, d),
            st("kp"), st("vp"), st("cp"), st("fp"),
            st("ks"), st("vs"), st("cs"), st("fs"))
```

```python
import functools
import math

import jax
import jax.numpy as jnp
import numpy as np
from jax import lax
from jax.experimental import pallas as pl
from jax.experimental.pallas import tpu as pltpu

F32 = jnp.float32
BF16 = jnp.bfloat16

HEAD_DIM = 128
CONV_W = 3
RMS_EPS = 1e-6
NEG_BIG = -1e30
LOG2E = 1.4426950408889634
LANES = 128
SUBLANES = 8
VMEM_LIMIT = 56 * 1024 * 1024
PARTIAL_SUMS = 1
FFN_SUB = 256


def _bf16_pieces(x, n):
    pieces = []
    for _ in range(n):
        p = float(np.asarray(x, np.float32).astype(BF16))
        pieces.append(p)
        x -= p
    return tuple(pieces)


SQRT_HEAD_DIM_PIECES = _bf16_pieces(HEAD_DIM ** 0.5, 3)


def _cparams(sem):
    return pltpu.CompilerParams(dimension_semantics=sem, vmem_limit_bytes=VMEM_LIMIT)


def _rms(x, w):
    ms = jnp.mean(x * x, axis=-1, keepdims=True)
    return x * lax.rsqrt(ms + RMS_EPS) * w


def _norm_kernel(x_ref, nw_ref, h_ref):
    h_ref[...] = _rms(x_ref[...], nw_ref[...]).astype(BF16)


def _norm(x, nw, tm):
    rows, d = x.shape
    return pl.pallas_call(
        _norm_kernel,
        grid=(rows // tm,),
        in_specs=[pl.BlockSpec((tm, d), lambda i: (i, 0)), pl.BlockSpec((1, d), lambda i: (0, 0))],
        out_specs=pl.BlockSpec((tm, d), lambda i: (i, 0)),
        out_shape=jax.ShapeDtypeStruct((rows, d), BF16),
        compiler_params=_cparams(("parallel",)),
        name="rmsnorm",
    )(x, nw)


def _proj_kernel(h_ref, w_ref, *o_refs, outs):
    y = jnp.dot(h_ref[...], w_ref[...], preferred_element_type=F32)
    if outs == "f32":
        o_refs[0][...] = y
    elif outs == "bf16":
        o_refs[0][...] = y.astype(BF16)
    else:
        o_refs[0][...] = y
        o_refs[1][...] = y.astype(BF16)


def _proj(h, w, col0, ncols, outs, tm, tn, name):
    rows, d = h.shape
    c0 = col0 // tn
    dts = {"f32": [F32], "bf16": [BF16], "both": [F32, BF16]}[outs]
    return pl.pallas_call(
        functools.partial(_proj_kernel, outs=outs),
        grid=(rows // tm, ncols // tn),
        in_specs=[
            pl.BlockSpec((tm, d), lambda i, j: (i, 0)),
            pl.BlockSpec((d, tn), lambda i, j: (0, c0 + j)),
        ],
        out_specs=[pl.BlockSpec((tm, tn), lambda i, j: (i, j)) for _ in dts],
        out_shape=[jax.ShapeDtypeStruct((rows, ncols), dt) for dt in dts],
        compiler_params=_cparams(("parallel", "parallel")),
        name=name,
    )(h, w)


def _conv_proj_kernel(h_ref, wu_ref, wb_ref, wc_ref, z_ref, bg_ref):
    h = h_ref[...]
    u = jnp.dot(h, wu_ref[...], preferred_element_type=F32)
    cg = jnp.dot(h, wc_ref[...], preferred_element_type=F32)
    z_ref[...] = cg * u
    bg_ref[...] = jnp.dot(h, wb_ref[...], preferred_element_type=F32)


def _conv_proj(h, w, d_conv, tm, tn, name):
    rows, d = h.shape
    nb = d_conv // tn
    return pl.pallas_call(
        _conv_proj_kernel,
        grid=(rows // tm, nb),
        in_specs=[
            pl.BlockSpec((tm, d), lambda i, j: (i, 0)),
            pl.BlockSpec((d, tn), lambda i, j: (0, j)),
            pl.BlockSpec((d, tn), lambda i, j: (0, nb + j)),
            pl.BlockSpec((d, tn), lambda i, j: (0, 2 * nb + j)),
        ],
        out_specs=[pl.BlockSpec((tm, tn), lambda i, j: (i, j))] * 2,
        out_shape=[jax.ShapeDtypeStruct((rows, d_conv), F32)] * 2,
        compiler_params=_cparams(("parallel", "parallel")),
        name=name,
    )(h, w, w, w)


def _lambda_value(lq1_ref, lk1_ref, lq2_ref, lk2_ref, lam_init):
    a = jnp.sum(lq1_ref[...] * lk1_ref[...], axis=-1, keepdims=True)
    b = jnp.sum(lq2_ref[...] * lk2_ref[...], axis=-1, keepdims=True)
    return jnp.exp(a) - jnp.exp(b) + lam_init


def _head_out(o, sub_w, lam_init):
    return _rms(o, sub_w) * (1.0 - lam_init)


def _flash_kernel(slope_ref, q_ref, k_ref, v_ref, lq1_ref, lk1_ref, lq2_ref, lk2_ref,
                  sub_ref, o_ref, acc1_ref, acc2_ref, *, tq, lam_init):
    h = pl.program_id(1)
    qi = pl.program_id(2)
    slope = slope_ref[h]
    scale2 = HEAD_DIM ** -0.5 * LOG2E

    lane = lax.broadcasted_iota(jnp.int32, (tq, HEAD_DIM), 1)
    pos = lax.broadcasted_iota(jnp.int32, (tq, HEAD_DIM), 0)
    n_pc = len(SQRT_HEAD_DIM_PIECES)
    low = pos & 1
    k_feat = jnp.where(lane < n_pc, pos - low, jnp.where(lane < 2 * n_pc, low, 0))
    aug_k = (k_feat.astype(F32) * slope).astype(BF16)
    q_feat = jnp.zeros((tq, HEAD_DIM), F32)
    for n, piece in enumerate(SQRT_HEAD_DIM_PIECES):
        q_feat = jnp.where((lane == n) | (lane == n + n_pc), piece, q_feat)
    aug_q = q_feat.astype(BF16)
    qa = [jnp.concatenate([q_ref[:, c0:c0 + HEAD_DIM], aug_q], axis=1) for c0 in (0, HEAD_DIM)]

    acc1_ref[...] = jnp.zeros_like(acc1_ref)
    acc2_ref[...] = jnp.zeros_like(acc2_ref)

    def scores(n, k, diag):
        ka = jnp.concatenate([k[:, n * HEAD_DIM:(n + 1) * HEAD_DIM], aug_k], axis=1)
        s = lax.dot_general(qa[n], ka, (((1,), (1,)), ((), ())), preferred_element_type=F32)
        s = s * scale2
        if diag:
            row = lax.broadcasted_iota(jnp.int32, (tq, tq), 0)
            col = lax.broadcasted_iota(jnp.int32, (tq, tq), 1)
            s = jnp.where(row >= col, s, NEG_BIG)
        return s

    def softmax_update(s, m, l, c):
        m_new = jnp.maximum(m, jnp.max(s, axis=-1, keepdims=True) + c)
        alpha = jnp.exp2(m - m_new)
        p = jnp.exp2(s - (m_new - c))
        return m_new, alpha * l + jnp.sum(p, axis=-1, keepdims=True), alpha, p.astype(BF16)

    def step(ki, carry, diag):
        m1, l1, m2, l2 = carry
        off = pl.multiple_of(ki * tq, tq)
        k = k_ref[pl.ds(off, tq), :]
        v = v_ref[pl.ds(off, tq), :]
        c = (slope * LOG2E) * ((ki - qi) * tq).astype(F32)
        s1 = scores(0, k, diag)
        s2 = scores(1, k, diag)
        m1, l1, alpha1, p1 = softmax_update(s1, m1, l1, c)
        m2, l2, alpha2, p2 = softmax_update(s2, m2, l2, c)
        acc1_ref[...] = alpha1 * acc1_ref[...] + jnp.dot(p1, v, preferred_element_type=F32)
        acc2_ref[...] = alpha2 * acc2_ref[...] + jnp.dot(p2, v, preferred_element_type=F32)
        return m1, l1, m2, l2

    m0 = jnp.full((tq, 1), NEG_BIG, F32)
    l0 = jnp.zeros((tq, 1), F32)
    carry = lax.fori_loop(0, qi, lambda ki, cr: step(ki, cr, False), (m0, l0, m0, l0))
    m1, l1, m2, l2 = step(qi, carry, True)

    lam = _lambda_value(lq1_ref, lk1_ref, lq2_ref, lk2_ref, lam_init)
    o = acc1_ref[...] / l1 - lam * (acc2_ref[...] / l2)
    o_ref[...] = _head_out(o, sub_ref[...], lam_init).astype(BF16)


def _flash_attention(q, k, v, slopes, lams, sub_w, batch, seq, n_heads, lam_init, tq=512):
    kd = 2 * HEAD_DIM
    nq = seq // tq
    lam_spec = pl.BlockSpec((1, HEAD_DIM), lambda b, h, i: (0, 0))
    return pl.pallas_call(
        functools.partial(_flash_kernel, tq=tq, lam_init=lam_init),
        grid=(batch, n_heads, nq),
        in_specs=[
            pl.BlockSpec(memory_space=pltpu.SMEM),
            pl.BlockSpec((tq, kd), lambda b, h, i: (b * nq + i, h)),
            pl.BlockSpec((seq, kd), lambda b, h, i: (b, h)),
            pl.BlockSpec((seq, kd), lambda b, h, i: (b, h)),
            lam_spec, lam_spec, lam_spec, lam_spec,
            pl.BlockSpec((1, kd), lambda b, h, i: (0, 0)),
        ],
        out_specs=pl.BlockSpec((tq, kd), lambda b, h, i: (b * nq + i, h)),
        out_shape=jax.ShapeDtypeStruct(q.shape, BF16),
        scratch_shapes=[pltpu.VMEM((tq, kd), F32), pltpu.VMEM((tq, kd), F32)],
        compiler_params=_cparams(("parallel", "parallel", "arbitrary")),
        name="prompt_attention",
    )(slopes, q, k, v, *lams, sub_w)


def _decode_kernel(pt_ref, q_ref, kn_ref, vn_ref, lq1_ref, lk1_ref, lq2_ref, lk2_ref,
                   sub_ref, slope_ref, tbl_ref, *rest, pages_per_step, page, n_pages,
                   lam_init, tokens_per_iter):
    del pt_ref
    k_refs = rest[:pages_per_step]
    v_refs = rest[pages_per_step:2 * pages_per_step]
    o_ref, m_ref, l_ref, acc_ref, s_even_ref, s_odd_ref = rest[2 * pages_per_step:]
    s_refs = (s_even_ref, s_odd_ref)
    g = pl.program_id(1)
    past = n_pages * page
    n_iter = page // tokens_per_iter

    @pl.when(g == 0)
    def _():
        m_ref[...] = jnp.full_like(m_ref, NEG_BIG)
        l_ref[...] = jnp.zeros_like(l_ref)
        acc_ref[...] = jnp.zeros_like(acc_ref)

    qv = q_ref[0] * (HEAD_DIM ** -0.5 * LOG2E)
    q1 = qv[:, :HEAD_DIM]
    q2 = qv[:, HEAD_DIM:]
    slope2 = slope_ref[...]

    def rowsum(x):
        return jnp.broadcast_to(jnp.sum(x, axis=-1, keepdims=True), x.shape)

    def twice(p):
        return jnp.concatenate([p, p], axis=-1)

    def run(score_page, acc_page, shifts):
        if score_page is not None:
            k_ref, sa_ref = k_refs[score_page], s_refs[score_page % 2]
        if acc_page is not None:
            v_ref, sb_ref = v_refs[acc_page], s_refs[acc_page % 2]

        def body(it, carry):
            mx1, mx2, parts = carry
            parts = [list(p) for p in parts]
            base = it * tokens_per_iter
            for tt in range(tokens_per_iter):
                t = base + tt
                if score_page is not None:
                    kt = k_ref[t]
                    tb = tbl_ref[t]
                    u1 = rowsum(kt[:, :HEAD_DIM] * q1) + tb
                    u2 = rowsum(kt[:, HEAD_DIM:] * q2) + tb
                    sa_ref[t, 0] = u1
                    sa_ref[t, 1] = u2
                    mx1 = jnp.maximum(mx1, u1)
                    mx2 = jnp.maximum(mx2, u2)
                if acc_page is not None:
                    p1 = jnp.exp2(sb_ref[t, 0] - shifts[0])
                    p2 = jnp.exp2(sb_ref[t, 1] - shifts[1])
                    vt = v_ref[t]
                    part = parts[tt % PARTIAL_SUMS]
                    part[0] = part[0] + p1
                    part[1] = part[1] + p2
                    part[2] = part[2] + twice(p1) * vt
                    part[3] = part[3] + twice(p2) * vt
            return mx1, mx2, tuple(tuple(p) for p in parts)

        neg = jnp.full((SUBLANES, LANES), NEG_BIG, F32)
        zero = (jnp.zeros_like(l_ref[0]), jnp.zeros_like(l_ref[0]),
                jnp.zeros_like(acc_ref[0]), jnp.zeros_like(acc_ref[0]))
        first = (l_ref[0], l_ref[1], acc_ref[0], acc_ref[1])
        if acc_page is None:
            parts0 = ()
        else:
            parts0 = (first,) + (zero,) * (PARTIAL_SUMS - 1)
        mx1, mx2, parts = lax.fori_loop(0, n_iter, body, (neg, neg, parts0))
        if acc_page is not None:
            l_ref[0] = sum(p[0] for p in parts[1:]) + parts[0][0]
            l_ref[1] = sum(p[1] for p in parts[1:]) + parts[0][1]
            acc_ref[0] = sum(p[2] for p in parts[1:]) + parts[0][2]
            acc_ref[1] = sum(p[3] for p in parts[1:]) + parts[0][3]
        return mx1, mx2

    def new_max(c, m_cand):
        m_new = jnp.maximum(m_ref[c], m_cand)
        alpha = jnp.exp2(m_ref[c] - m_new)
        m_ref[c] = m_new
        l_ref[c] = alpha * l_ref[c]
        acc_ref[c] = twice(alpha) * acc_ref[c]
        return m_new

    def page_shifts(r, mx1, mx2):
        first_pos = (g * pages_per_step + r) * page
        c = slope2 * (first_pos - past).astype(F32)
        return new_max(0, mx1 + c) - c, new_max(1, mx2 + c) - c

    mx = run(0, None, None)
    for r in range(pages_per_step):
        shifts = page_shifts(r, *mx)
        if r + 1 < pages_per_step:
            mx = run(r + 1, r, shifts)
        else:
            run(None, r, shifts)

    @pl.when(g == pl.num_programs(1) - 1)
    def _():
        kn = kn_ref[0]
        vn = vn_ref[0]
        s1 = rowsum(kn[:, :HEAD_DIM] * q1)
        s2 = rowsum(kn[:, HEAD_DIM:] * q2)
        p1 = jnp.exp2(s1 - new_max(0, s1))
        p2 = jnp.exp2(s2 - new_max(1, s2))
        o1 = (acc_ref[0] + twice(p1) * vn) / twice(l_ref[0] + p1)
        o2 = (acc_ref[1] + twice(p2) * vn) / twice(l_ref[1] + p2)
        lam = _lambda_value(lq1_ref, lk1_ref, lq2_ref, lk2_ref, lam_init)
        o_ref[0] = _head_out(o1 - lam * o2, sub_ref[...], lam_init)


def _decode_attention(q, k_new, v_new, cache_k, cache_v, page_table, layer, slopes, lams,
                      sub_w, lam_init, pages_per_step=8, tokens_per_iter=128):
    nb, n_heads, kd = q.shape
    n_pages = page_table.shape[1]
    page = cache_k.shape[2]
    steps = n_pages // pages_per_step
    slope2 = jnp.broadcast_to((slopes * LOG2E)[:, None], (n_heads, LANES))
    tok_bias = jnp.arange(page, dtype=F32)[:, None, None] * slope2[None]

    tok_spec = pl.BlockSpec((1, n_heads, kd), lambda b, g, pt: (b, 0, 0))
    lam_spec = pl.BlockSpec((1, HEAD_DIM), lambda b, g, pt: (0, 0))

    def page_spec(r):
        return pl.BlockSpec(
            (None, None, page, n_heads, kd),
            lambda b, g, pt: (layer, pt[b, g * pages_per_step + r], 0, 0, 0))

    page_specs = [page_spec(r) for r in range(pages_per_step)]
    grid_spec = pltpu.PrefetchScalarGridSpec(
        num_scalar_prefetch=1,
        grid=(nb, steps),
        in_specs=[tok_spec, tok_spec, tok_spec, lam_spec, lam_spec, lam_spec, lam_spec,
                  pl.BlockSpec((1, kd), lambda b, g, pt: (0, 0)),
                  pl.BlockSpec((n_heads, LANES), lambda b, g, pt: (0, 0)),
                  pl.BlockSpec((page, n_heads, LANES), lambda b, g, pt: (0, 0, 0))]
                 + page_specs + page_specs,
        out_specs=tok_spec,
        scratch_shapes=[
            pltpu.VMEM((2, n_heads, LANES), F32),
            pltpu.VMEM((2, n_heads, LANES), F32),
            pltpu.VMEM((2, n_heads, kd), F32),
            pltpu.VMEM((page, 2, n_heads, LANES), F32),
            pltpu.VMEM((page, 2, n_heads, LANES), F32),
        ],
    )
    return pl.pallas_call(
        functools.partial(_decode_kernel, pages_per_step=pages_per_step, page=page,
                          n_pages=n_pages, lam_init=lam_init, tokens_per_iter=tokens_per_iter),
        grid_spec=grid_spec,
        out_shape=jax.ShapeDtypeStruct((nb, n_heads, kd), F32),
        compiler_params=_cparams(("parallel", "arbitrary")),
        name="sample_attention",
    )(page_table, q, k_new, v_new, *lams, sub_w, slope2, tok_bias,
      *([cache_k] * pages_per_step), *([cache_v] * pages_per_step))


def _merge_kernel(*refs, mode, tm, tiles_per_seq):
    if mode == "seq":
        (z_ref, halo_ref, bg_ref, o_ref, ga_ref, gb_ref, x_ref, cw_ref, wc_ref, wa_ref,
         wo_ref, pw_ref, fw_ref, out_ref, h_ref, zbuf_ref) = refs
        i = pl.program_id(0)
        z = z_ref[...]
        starts_seq = i % tiles_per_seq == 0
        zbuf_ref[0:SUBLANES, :] = jnp.where(starts_seq, 0.0, halo_ref[...])
        zbuf_ref[SUBLANES:, :] = z
        z1 = zbuf_ref[SUBLANES - 1:SUBLANES - 1 + tm, :]
        z2 = zbuf_ref[SUBLANES - 2:SUBLANES - 2 + tm, :]
    else:
        (z_ref, st0_ref, st1_ref, bg_ref, o_ref, ga_ref, gb_ref, x_ref, cw_ref, wc_ref,
         wa_ref, wo_ref, pw_ref, fw_ref, out_ref, h_ref) = refs
        z = z_ref[...]
        z2 = st0_ref[...]
        z1 = st1_ref[...]
    cw = cw_ref[...]
    zc = z2 * cw[0:1] + z1 * cw[1:2] + z * cw[2:3]
    y_conv = jnp.dot((bg_ref[...] * zc).astype(BF16), wc_ref[...], preferred_element_type=F32)
    y_attn = jnp.dot(o_ref[...], wa_ref[...], preferred_element_type=F32)
    merged = jax.nn.sigmoid(ga_ref[...]) * y_conv + jax.nn.sigmoid(gb_ref[...]) * y_attn
    mo = jnp.dot(merged.astype(BF16), wo_ref[...], preferred_element_type=F32)
    x_new = x_ref[...] + _rms(mo, pw_ref[...])
    out_ref[...] = x_new
    h_ref[...] = _rms(x_new, fw_ref[...]).astype(BF16)


def _merge(z, prev, bg, o, gates, x, conv_w, wc, wa, wo, post_w, ffn_w, mode, tm, seq, name):
    rows, d = x.shape
    dc = z.shape[1]
    row = lambda w: pl.BlockSpec((tm, w), lambda i: (i, 0))
    const = lambda a: pl.BlockSpec(a.shape, lambda i: (0, 0), pipeline_mode=pl.Buffered(1))
    if mode == "seq":
        hb = tm // SUBLANES
        prev_specs = [pl.BlockSpec((SUBLANES, dc), lambda i: (jnp.maximum(i * hb - 1, 0), 0))]
        prev_args = [z]
        scratch = [pltpu.VMEM((tm + SUBLANES, dc), F32)]
        tiles_per_seq = seq // tm
    else:
        prev_specs = [row(dc), row(dc)]
        prev_args = list(prev)
        scratch = []
        tiles_per_seq = 1
    return pl.pallas_call(
        functools.partial(_merge_kernel, mode=mode, tm=tm, tiles_per_seq=tiles_per_seq),
        grid=(rows // tm,),
        in_specs=[row(dc)] + prev_specs + [
            row(dc), row(d),
            pl.BlockSpec((tm, d), lambda i: (i, 0)),
            pl.BlockSpec((tm, d), lambda i: (i, 1)),
            row(d), const(conv_w), const(wc), const(wa), const(wo), const(post_w), const(ffn_w)],
        out_specs=[row(d), row(d)],
        out_shape=[jax.ShapeDtypeStruct((rows, d), F32), jax.ShapeDtypeStruct((rows, d), BF16)],
        scratch_shapes=scratch,
        compiler_params=_cparams(("arbitrary",)),
        name=name,
    )(z, *prev_args, bg, o, gates, gates, x, conv_w, wc, wa, wo, post_w, ffn_w)


def _ffn_kernel(*refs, mode, tm, tiles_per_seq):
    if mode == "seq":
        (x_ref, h_ref, wg_ref, wv_ref, cwg_ref, cwv_ref, wd_ref, pw_ref,
         out_ref, sg_ref, sv_ref, acc_ref, cg_ref, cv_ref, ug_ref, uv_ref) = refs
    else:
        (x_ref, h_ref, wg_ref, wv_ref, cwg_ref, cwv_ref, wd_ref, pw_ref,
         g0_ref, g1_ref, v0_ref, v1_ref, out_ref, sg_ref, sv_ref, acc_ref) = refs
    i = pl.program_id(0)
    j = pl.program_id(1)

    @pl.when(j == 0)
    def _():
        acc_ref[...] = jnp.zeros_like(acc_ref)
        if mode == "seq":
            @pl.when(i % tiles_per_seq == 0)
            def _():
                cg_ref[...] = jnp.zeros_like(cg_ref)
                cv_ref[...] = jnp.zeros_like(cv_ref)

    h = h_ref[...]
    tf = wd_ref.shape[0]

    def seq_up(cs, w_ref, carry_ref, ubuf_ref, state_ref):
        up = jnp.dot(h, w_ref[:, cs], preferred_element_type=F32)
        ubuf_ref[0:SUBLANES, cs] = carry_ref[j, :, cs]
        ubuf_ref[SUBLANES:, cs] = up
        carry_ref[j, :, cs] = up[tm - SUBLANES:, :]
        state_ref[:, cs] = up[tm - (CONV_W - 1):, :]

    def seq_conv(cs, cw_ref, ubuf_ref):
        cw = cw_ref[:, cs]
        u0 = ubuf_ref[SUBLANES:SUBLANES + tm, cs]
        u1 = ubuf_ref[SUBLANES - 1:SUBLANES - 1 + tm, cs]
        u2 = ubuf_ref[SUBLANES - 2:SUBLANES - 2 + tm, cs]
        return u2 * cw[0:1] + u1 * cw[1:2] + u0 * cw[2:3]

    def state_branch(cs, w_ref, cw_ref, p0_ref, p1_ref, state_ref):
        up = jnp.dot(h, w_ref[:, cs], preferred_element_type=F32)
        state_ref[:, cs] = up
        cw = cw_ref[:, cs]
        return p0_ref[:, cs] * cw[0:1] + p1_ref[:, cs] * cw[1:2] + up * cw[2:3]

    chunks = [slice(c0, c0 + FFN_SUB) for c0 in range(0, tf, FFN_SUB)]
    if mode == "seq":
        for cs in chunks:
            seq_up(cs, wg_ref, cg_ref, ug_ref, sg_ref)
            seq_up(cs, wv_ref, cv_ref, uv_ref, sv_ref)
    down = None
    for cs in chunks:
        if mode == "seq":
            gate = seq_conv(cs, cwg_ref, ug_ref)
            val = seq_conv(cs, cwv_ref, uv_ref)
        else:
            gate = state_branch(cs, wg_ref, cwg_ref, g0_ref, g1_ref, sg_ref)
            val = state_branch(cs, wv_ref, cwv_ref, v0_ref, v1_ref, sv_ref)
        act = (gate * jax.nn.sigmoid(gate)) * val
        part = jnp.dot(act.astype(BF16), wd_ref[cs, :], preferred_element_type=F32)
        down = part if down is None else down + part
    acc_ref[...] += down

    @pl.when(j == pl.num_programs(1) - 1)
    def _():
        out_ref[...] = x_ref[...] + _rms(acc_ref[...], pw_ref[...])


def _ffn(x, h, w_up, conv_w, w_down, post_w, prev, mode, tm, seq, tf, name):
    rows, d = x.shape
    d_ff = w_down.shape[0]
    nj = d_ff // tf
    common = [
        pl.BlockSpec((tm, d), lambda i, j: (i, 0)),
        pl.BlockSpec((tm, d), lambda i, j: (i, 0)),
        pl.BlockSpec((d, tf), lambda i, j: (0, j)),
        pl.BlockSpec((d, tf), lambda i, j: (0, nj + j)),
        pl.BlockSpec((CONV_W, tf), lambda i, j: (0, j)),
        pl.BlockSpec((CONV_W, tf), lambda i, j: (0, nj + j)),
        pl.BlockSpec((tf, d), lambda i, j: (j, 0)),
        pl.BlockSpec((1, d), lambda i, j: (0, 0)),
    ]
    args = [x, h, w_up, w_up, conv_w, conv_w, w_down, post_w]
    scratch = [pltpu.VMEM((tm, d), F32)]
    if mode == "seq":
        tiles_per_seq = seq // tm
        st_spec = pl.BlockSpec((None, CONV_W - 1, tf), lambda i, j: (i, 0, j))
        st_shape = jax.ShapeDtypeStruct((rows // tm, CONV_W - 1, d_ff), F32)
        in_specs = common
        scratch += [pltpu.VMEM((nj, SUBLANES, tf), F32), pltpu.VMEM((nj, SUBLANES, tf), F32),
                    pltpu.VMEM((tm + SUBLANES, tf), F32), pltpu.VMEM((tm + SUBLANES, tf), F32)]
    else:
        tiles_per_seq = 1
        st_spec = pl.BlockSpec((tm, tf), lambda i, j: (i, j))
        st_shape = jax.ShapeDtypeStruct((rows, d_ff), F32)
        pspec = pl.BlockSpec((tm, tf), lambda i, j: (i, j))
        in_specs = common + [pspec] * 4
        args += list(prev)
    return pl.pallas_call(
        functools.partial(_ffn_kernel, mode=mode, tm=tm, tiles_per_seq=tiles_per_seq),
        grid=(rows // tm, nj),
        in_specs=in_specs,
        out_specs=[pl.BlockSpec((tm, d), lambda i, j: (i, 0)), st_spec, st_spec],
        out_shape=[jax.ShapeDtypeStruct((rows, d), F32), st_shape, st_shape],
        scratch_shapes=scratch,
        compiler_params=_cparams(("arbitrary", "arbitrary")),
        name=name,
    )(*args)


def kernel(x_prompt, x_sample, cache_k, cache_v, page_table, state_conv, state_ffn_conv,
           pre_mix_w, w_in, conv_w, w_conv_out, lambda_q1, lambda_k1, lambda_q2, lambda_k2,
           subln_w, w_attn_out, w_o, post_mix_w, pre_ffn_w, w_up, ffn_conv_w, w_down, post_ffn_w):
    batch, seq, d = x_prompt.shape
    nb = x_sample.shape[0]
    depth = w_in.shape[0]
    n_heads = cache_k.shape[3]
    kd = cache_k.shape[4]
    d_conv = conv_w.shape[2]
    d_ff = w_down.shape[1]
    rows = batch * seq
    qkv = n_heads * kd
    c_q, c_k, c_v, c_g = 3 * d_conv, 3 * d_conv + qkv, 3 * d_conv + 2 * qkv, 3 * d_conv + 3 * qkv
    slopes = jnp.exp2(-8.0 * jnp.arange(1, n_heads + 1, dtype=F32) / n_heads)

    xp = x_prompt.reshape(rows, d)
    xs = x_sample.reshape(nb, d)
    outs = {n: [] for n in ("kp", "vp", "cp", "fp", "ks", "vs", "cs", "fs")}
    tm, tn = 1024, 1024
    tm_merge, tm_ffn, tf = 256, 512, 512

    for l in range(depth):
        lam_init = 0.8 - 0.6 * math.exp(-0.3 * l)
        w_in_b = w_in[l].astype(BF16)
        wc_b = w_conv_out[l].astype(BF16)
        wa_b = w_attn_out[l].astype(BF16)
        wo_b = w_o[l].astype(BF16)
        wu_b = w_up[l].astype(BF16)
        wd_b = w_down[l].astype(BF16)
        row2 = lambda a: a[l].reshape(1, -1)
        lams = [row2(lambda_q1), row2(lambda_k1), row2(lambda_q2), row2(lambda_k2)]
        pre_w, post_w = row2(pre_mix_w), row2(post_mix_w)
        pre_f, post_f, sub_w = row2(pre_ffn_w), row2(post_ffn_w), row2(subln_w)

        hp = _norm(xp, pre_w, 512)
        z, bg = _conv_proj(hp, w_in_b, d_conv, tm, 512, "p_conv_proj")
        (q_b,) = _proj(hp, w_in_b, c_q, qkv, "bf16", tm, tn, "p_q_proj")
        k_f, k_b = _proj(hp, w_in_b, c_k, qkv, "both", tm, tn, "p_k_proj")
        v_f, v_b = _proj(hp, w_in_b, c_v, qkv, "both", tm, tn, "p_v_proj")
        (gates,) = _proj(hp, w_in_b, c_g, 2 * d, "f32", tm, tn, "p_gate_proj")
        o = _flash_attention(q_b, k_b, v_b, slopes, lams, sub_w, batch, seq, n_heads, lam_init)
        xp, hf = _merge(z, None, bg, o, gates, xp, conv_w[l], wc_b, wa_b, wo_b, post_w, pre_f,
                        "seq", tm_merge, seq, "p_merge")
        xp, fg, fv = _ffn(xp, hf, wu_b, ffn_conv_w[l], wd_b, post_f, None, "seq", tm_ffn, seq, tf,
                          "p_ffn")
        last = seq // tm_ffn - 1
        outs["kp"].append(k_f.reshape(batch, seq, n_heads, kd))
        outs["vp"].append(v_f.reshape(batch, seq, n_heads, kd))
        outs["cp"].append(z.reshape(batch, seq, d_conv)[:, seq - (CONV_W - 1):])
        outs["fp"].append(jnp.concatenate([fg, fv], axis=-1)[last::seq // tm_ffn])

        hs = _norm(xs, pre_w, nb)
        zs, bgs = _conv_proj(hs, w_in_b, d_conv, nb, 512, "s_conv_proj")
        (qkv_s,) = _proj(hs, w_in_b, c_q, 3 * qkv, "f32", nb, tn, "s_qkv_proj")
        (gates_s,) = _proj(hs, w_in_b, c_g, 2 * d, "f32", nb, tn, "s_gate_proj")
        q_s = qkv_s[:, :qkv].reshape(nb, n_heads, kd)
        k_s = qkv_s[:, qkv:2 * qkv].reshape(nb, n_heads, kd)
        v_s = qkv_s[:, 2 * qkv:].reshape(nb, n_heads, kd)
        o_s = _decode_attention(q_s, k_s, v_s, cache_k, cache_v, page_table, l, slopes, lams,
                                sub_w, lam_init)
        sc = state_conv[l]
        xs, hfs = _merge(zs, (sc[:, 0], sc[:, 1]), bgs, o_s.reshape(nb, qkv).astype(BF16), gates_s,
                         xs, conv_w[l], wc_b, wa_b, wo_b, post_w, pre_f, "state", nb, 1, "s_merge")
        sf = state_ffn_conv[l]
        prev = (sf[:, 0, :d_ff], sf[:, 1, :d_ff], sf[:, 0, d_ff:], sf[:, 1, d_ff:])
        xs, ug, uv = _ffn(xs, hfs, wu_b, ffn_conv_w[l], wd_b, post_f, prev, "state", nb, 1, tf,
                          "s_ffn")
        outs["ks"].append(k_s.reshape(nb, 1, n_heads, kd))
        outs["vs"].append(v_s.reshape(nb, 1, n_heads, kd))
        outs["cs"].append(jnp.stack([sc[:, 1], zs], axis=1))
        outs["fs"].append(jnp.stack([sf[:, 1], jnp.concatenate([ug, uv], axis=-1)], axis=1))

    st = lambda n: jnp.stack(outs[n])
    return (xp.reshape(batch, seq, d), xs.reshape(nb, 1, d),
            st("kp"), st("vp"), st("cp"), st("fp"),
            st("ks"), st("vs"), st("cs"), st("fs"))
```

```python
import functools
import math

import jax
import jax.numpy as jnp
import numpy as np
from jax import lax
from jax.experimental import pallas as pl
from jax.experimental.pallas import tpu as pltpu

F32 = jnp.float32
BF16 = jnp.bfloat16

HEAD_DIM = 128
CONV_W = 3
RMS_EPS = 1e-6
NEG_BIG = -1e30
LOG2E = 1.4426950408889634
LANES = 128
SUBLANES = 8
VMEM_LIMIT = 56 * 1024 * 1024
PARTIAL_SUMS = 1
FFN_SUB = 256


def _bf16_pieces(x, n):
    pieces = []
    for _ in range(n):
        p = float(np.asarray(x, np.float32).astype(BF16))
        pieces.append(p)
        x -= p
    return tuple(pieces)


SQRT_HEAD_DIM_PIECES = _bf16_pieces(HEAD_DIM ** 0.5, 3)


def _cparams(sem):
    return pltpu.CompilerParams(dimension_semantics=sem, vmem_limit_bytes=VMEM_LIMIT)


def _rms(x, w):
    ms = jnp.mean(x * x, axis=-1, keepdims=True)
    return x * lax.rsqrt(ms + RMS_EPS) * w


def _norm_kernel(x_ref, nw_ref, h_ref):
    h_ref[...] = _rms(x_ref[...], nw_ref[...]).astype(BF16)


def _norm(x, nw, tm):
    rows, d = x.shape
    return pl.pallas_call(
        _norm_kernel,
        grid=(rows // tm,),
        in_specs=[pl.BlockSpec((tm, d), lambda i: (i, 0)), pl.BlockSpec((1, d), lambda i: (0, 0))],
        out_specs=pl.BlockSpec((tm, d), lambda i: (i, 0)),
        out_shape=jax.ShapeDtypeStruct((rows, d), BF16),
        compiler_params=_cparams(("parallel",)),
        name="rmsnorm",
    )(x, nw)


def _proj_kernel(h_ref, w_ref, *o_refs, outs):
    y = jnp.dot(h_ref[...], w_ref[...], preferred_element_type=F32)
    if outs == "f32":
        o_refs[0][...] = y
    elif outs == "bf16":
        o_refs[0][...] = y.astype(BF16)
    else:
        o_refs[0][...] = y
        o_refs[1][...] = y.astype(BF16)


def _proj(h, w, col0, ncols, outs, tm, tn, name):
    rows, d = h.shape
    c0 = col0 // tn
    dts = {"f32": [F32], "bf16": [BF16], "both": [F32, BF16]}[outs]
    return pl.pallas_call(
        functools.partial(_proj_kernel, outs=outs),
        grid=(rows // tm, ncols // tn),
        in_specs=[
            pl.BlockSpec((tm, d), lambda i, j: (i, 0)),
            pl.BlockSpec((d, tn), lambda i, j: (0, c0 + j)),
        ],
        out_specs=[pl.BlockSpec((tm, tn), lambda i, j: (i, j)) for _ in dts],
        out_shape=[jax.ShapeDtypeStruct((rows, ncols), dt) for dt in dts],
        compiler_params=_cparams(("parallel", "parallel")),
        name=name,
    )(h, w)


def _conv_proj_kernel(h_ref, wu_ref, wb_ref, wc_ref, z_ref, bg_ref):
    h = h_ref[...]
    u = jnp.dot(h, wu_ref[...], preferred_element_type=F32)
    cg = jnp.dot(h, wc_ref[...], preferred_element_type=F32)
    z_ref[...] = cg * u
    bg_ref[...] = jnp.dot(h, wb_ref[...], preferred_element_type=F32)


def _conv_proj(h, wu, wb, wc, tm, tn, name):
    rows, d = h.shape
    d_conv = wu.shape[1]
    w_spec = pl.BlockSpec((d, tn), lambda i, j: (0, j))
    return pl.pallas_call(
        _conv_proj_kernel,
        grid=(rows // tm, d_conv // tn),
        in_specs=[pl.BlockSpec((tm, d), lambda i, j: (i, 0)), w_spec, w_spec, w_spec],
        out_specs=[pl.BlockSpec((tm, tn), lambda i, j: (i, j))] * 2,
        out_shape=[jax.ShapeDtypeStruct((rows, d_conv), F32)] * 2,
        compiler_params=_cparams(("parallel", "parallel")),
        name=name,
    )(h, wu, wb, wc)


def _lambda_value(lq1_ref, lk1_ref, lq2_ref, lk2_ref, lam_init):
    a = jnp.sum(lq1_ref[...] * lk1_ref[...], axis=-1, keepdims=True)
    b = jnp.sum(lq2_ref[...] * lk2_ref[...], axis=-1, keepdims=True)
    return jnp.exp(a) - jnp.exp(b) + lam_init


def _head_out(o, sub_w, lam_init):
    return _rms(o, sub_w) * (1.0 - lam_init)


def _flash_kernel(slope_ref, q_ref, k_ref, v_ref, lq1_ref, lk1_ref, lq2_ref, lk2_ref,
                  sub_ref, o_ref, acc1_ref, acc2_ref, *, tq, lam_init):
    h = pl.program_id(1)
    qi = pl.program_id(2)
    slope = slope_ref[h]
    scale2 = HEAD_DIM ** -0.5 * LOG2E

    lane = lax.broadcasted_iota(jnp.int32, (tq, HEAD_DIM), 1)
    pos = lax.broadcasted_iota(jnp.int32, (tq, HEAD_DIM), 0)
    n_pc = len(SQRT_HEAD_DIM_PIECES)
    low = pos & 1
    k_feat = jnp.where(lane < n_pc, pos - low, jnp.where(lane < 2 * n_pc, low, 0))
    aug_k = (k_feat.astype(F32) * slope).astype(BF16)
    q_feat = jnp.zeros((tq, HEAD_DIM), F32)
    for n, piece in enumerate(SQRT_HEAD_DIM_PIECES):
        q_feat = jnp.where((lane == n) | (lane == n + n_pc), piece, q_feat)
    aug_q = q_feat.astype(BF16)
    qa = [jnp.concatenate([q_ref[:, c0:c0 + HEAD_DIM], aug_q], axis=1) for c0 in (0, HEAD_DIM)]

    acc1_ref[...] = jnp.zeros_like(acc1_ref)
    acc2_ref[...] = jnp.zeros_like(acc2_ref)

    def scores(n, k, diag):
        ka = jnp.concatenate([k[:, n * HEAD_DIM:(n + 1) * HEAD_DIM], aug_k], axis=1)
        s = lax.dot_general(qa[n], ka, (((1,), (1,)), ((), ())), preferred_element_type=F32)
        s = s * scale2
        if diag:
            row = lax.broadcasted_iota(jnp.int32, (tq, tq), 0)
            col = lax.broadcasted_iota(jnp.int32, (tq, tq), 1)
            s = jnp.where(row >= col, s, NEG_BIG)
        return s

    def softmax_update(s, m, l, c):
        m_new = jnp.maximum(m, jnp.max(s, axis=-1, keepdims=True) + c)
        alpha = jnp.exp2(m - m_new)
        p = jnp.exp2(s - (m_new - c))
        return m_new, alpha * l + jnp.sum(p, axis=-1, keepdims=True), alpha, p.astype(BF16)

    def step(ki, carry, diag):
        m1, l1, m2, l2 = carry
        off = pl.multiple_of(ki * tq, tq)
        k = k_ref[pl.ds(off, tq), :]
        v = v_ref[pl.ds(off, tq), :]
        c = (slope * LOG2E) * ((ki - qi) * tq).astype(F32)
        s1 = scores(0, k, diag)
        s2 = scores(1, k, diag)
        m1, l1, alpha1, p1 = softmax_update(s1, m1, l1, c)
        m2, l2, alpha2, p2 = softmax_update(s2, m2, l2, c)
        acc1_ref[...] = alpha1 * acc1_ref[...] + jnp.dot(p1, v, preferred_element_type=F32)
        acc2_ref[...] = alpha2 * acc2_ref[...] + jnp.dot(p2, v, preferred_element_type=F32)
        return m1, l1, m2, l2

    m0 = jnp.full((tq, 1), NEG_BIG, F32)
    l0 = jnp.zeros((tq, 1), F32)
    carry = lax.fori_loop(0, qi, lambda ki, cr: step(ki, cr, False), (m0, l0, m0, l0))
    m1, l1, m2, l2 = step(qi, carry, True)

    lam = _lambda_value(lq1_ref, lk1_ref, lq2_ref, lk2_ref, lam_init)
    o = acc1_ref[...] / l1 - lam * (acc2_ref[...] / l2)
    o_ref[...] = _head_out(o, sub_ref[...], lam_init).astype(BF16)


def _flash_attention(q, k, v, slopes, lams, sub_w, batch, seq, n_heads, lam_init, tq=512):
    kd = 2 * HEAD_DIM
    nq = seq // tq
    lam_spec = pl.BlockSpec((1, HEAD_DIM), lambda b, h, i: (0, 0))
    return pl.pallas_call(
        functools.partial(_flash_kernel, tq=tq, lam_init=lam_init),
        grid=(batch, n_heads, nq),
        in_specs=[
            pl.BlockSpec(memory_space=pltpu.SMEM),
            pl.BlockSpec((tq, kd), lambda b, h, i: (b * nq + i, h)),
            pl.BlockSpec((seq, kd), lambda b, h, i: (b, h)),
            pl.BlockSpec((seq, kd), lambda b, h, i: (b, h)),
            lam_spec, lam_spec, lam_spec, lam_spec,
            pl.BlockSpec((1, kd), lambda b, h, i: (0, 0)),
        ],
        out_specs=pl.BlockSpec((tq, kd), lambda b, h, i: (b * nq + i, h)),
        out_shape=jax.ShapeDtypeStruct(q.shape, BF16),
        scratch_shapes=[pltpu.VMEM((tq, kd), F32), pltpu.VMEM((tq, kd), F32)],
        compiler_params=_cparams(("parallel", "parallel", "arbitrary")),
        name="prompt_attention",
    )(slopes, q, k, v, *lams, sub_w)


def _decode_kernel(pt_ref, q_ref, kn_ref, vn_ref, lq1_ref, lk1_ref, lq2_ref, lk2_ref,
                   sub_ref, slope_ref, tbl_ref, *rest, pages_per_step, page, n_pages,
                   lam_init, tokens_per_iter):
    del pt_ref
    k_refs = rest[:pages_per_step]
    v_refs = rest[pages_per_step:2 * pages_per_step]
    o_ref, m_ref, l_ref, acc_ref, s_even_ref, s_odd_ref = rest[2 * pages_per_step:]
    s_refs = (s_even_ref, s_odd_ref)
    g = pl.program_id(1)
    past = n_pages * page
    n_iter = page // tokens_per_iter

    @pl.when(g == 0)
    def _():
        m_ref[...] = jnp.full_like(m_ref, NEG_BIG)
        l_ref[...] = jnp.zeros_like(l_ref)
        acc_ref[...] = jnp.zeros_like(acc_ref)

    qv = q_ref[0] * (HEAD_DIM ** -0.5 * LOG2E)
    q1 = qv[:, :HEAD_DIM]
    q2 = qv[:, HEAD_DIM:]
    slope2 = slope_ref[...]

    def rowsum(x):
        return jnp.broadcast_to(jnp.sum(x, axis=-1, keepdims=True), x.shape)

    def twice(p):
        return jnp.concatenate([p, p], axis=-1)

    def run(score_page, acc_page, shifts):
        if score_page is not None:
            k_ref, sa_ref = k_refs[score_page], s_refs[score_page % 2]
        if acc_page is not None:
            v_ref, sb_ref = v_refs[acc_page], s_refs[acc_page % 2]

        def body(it, carry):
            mx1, mx2, parts = carry
            parts = [list(p) for p in parts]
            base = it * tokens_per_iter
            for tt in range(tokens_per_iter):
                t = base + tt
                if score_page is not None:
                    kt = k_ref[t]
                    tb = tbl_ref[t]
                    u1 = rowsum(kt[:, :HEAD_DIM] * q1) + tb
                    u2 = rowsum(kt[:, HEAD_DIM:] * q2) + tb
                    sa_ref[t, 0] = u1
                    sa_ref[t, 1] = u2
                    mx1 = jnp.maximum(mx1, u1)
                    mx2 = jnp.maximum(mx2, u2)
                if acc_page is not None:
                    p1 = jnp.exp2(sb_ref[t, 0] - shifts[0])
                    p2 = jnp.exp2(sb_ref[t, 1] - shifts[1])
                    vt = v_ref[t]
                    part = parts[tt % PARTIAL_SUMS]
                    part[0] = part[0] + p1
                    part[1] = part[1] + p2
                    part[2] = part[2] + twice(p1) * vt
                    part[3] = part[3] + twice(p2) * vt
            return mx1, mx2, tuple(tuple(p) for p in parts)

        neg = jnp.full((SUBLANES, LANES), NEG_BIG, F32)
        zero = (jnp.zeros_like(l_ref[0]), jnp.zeros_like(l_ref[0]),
                jnp.zeros_like(acc_ref[0]), jnp.zeros_like(acc_ref[0]))
        first = (l_ref[0], l_ref[1], acc_ref[0], acc_ref[1])
        if acc_page is None:
            parts0 = ()
        else:
            parts0 = (first,) + (zero,) * (PARTIAL_SUMS - 1)
        mx1, mx2, parts = lax.fori_loop(0, n_iter, body, (neg, neg, parts0))
        if acc_page is not None:
            l_ref[0] = sum(p[0] for p in parts[1:]) + parts[0][0]
            l_ref[1] = sum(p[1] for p in parts[1:]) + parts[0][1]
            acc_ref[0] = sum(p[2] for p in parts[1:]) + parts[0][2]
            acc_ref[1] = sum(p[3] for p in parts[1:]) + parts[0][3]
        return mx1, mx2

    def new_max(c, m_cand):
        m_new = jnp.maximum(m_ref[c], m_cand)
        alpha = jnp.exp2(m_ref[c] - m_new)
        m_ref[c] = m_new
        l_ref[c] = alpha * l_ref[c]
        acc_ref[c] = twice(alpha) * acc_ref[c]
        return m_new

    def page_shifts(r, mx1, mx2):
        first_pos = (g * pages_per_step + r) * page
        c = slope2 * (first_pos - past).astype(F32)
        return new_max(0, mx1 + c) - c, new_max(1, mx2 + c) - c

    mx = run(0, None, None)
    for r in range(pages_per_step):
        shifts = page_shifts(r, *mx)
        if r + 1 < pages_per_step:
            mx = run(r + 1, r, shifts)
        else:
            run(None, r, shifts)

    @pl.when(g == pl.num_programs(1) - 1)
    def _():
        kn = kn_ref[0]
        vn = vn_ref[0]
        s1 = rowsum(kn[:, :HEAD_DIM] * q1)
        s2 = rowsum(kn[:, HEAD_DIM:] * q2)
        p1 = jnp.exp2(s1 - new_max(0, s1))
        p2 = jnp.exp2(s2 - new_max(1, s2))
        o1 = (acc_ref[0] + twice(p1) * vn) / twice(l_ref[0] + p1)
        o2 = (acc_ref[1] + twice(p2) * vn) / twice(l_ref[1] + p2)
        lam = _lambda_value(lq1_ref, lk1_ref, lq2_ref, lk2_ref, lam_init)
        o_ref[0] = _head_out(o1 - lam * o2, sub_ref[...], lam_init)


def _decode_attention(q, k_new, v_new, cache_k, cache_v, page_table, layer, slopes, lams,
                      sub_w, lam_init, pages_per_step=8, tokens_per_iter=128):
    nb, n_heads, kd = q.shape
    n_pages = page_table.shape[1]
    page = cache_k.shape[2]
    steps = n_pages // pages_per_step
    slope2 = jnp.broadcast_to((slopes * LOG2E)[:, None], (n_heads, LANES))
    tok_bias = jnp.arange(page, dtype=F32)[:, None, None] * slope2[None]

    tok_spec = pl.BlockSpec((1, n_heads, kd), lambda b, g, pt: (b, 0, 0))
    lam_spec = pl.BlockSpec((1, HEAD_DIM), lambda b, g, pt: (0, 0))

    def page_spec(r):
        return pl.BlockSpec(
            (None, None, page, n_heads, kd),
            lambda b, g, pt: (layer, pt[b, g * pages_per_step + r], 0, 0, 0))

    page_specs = [page_spec(r) for r in range(pages_per_step)]
    grid_spec = pltpu.PrefetchScalarGridSpec(
        num_scalar_prefetch=1,
        grid=(nb, steps),
        in_specs=[tok_spec, tok_spec, tok_spec, lam_spec, lam_spec, lam_spec, lam_spec,
                  pl.BlockSpec((1, kd), lambda b, g, pt: (0, 0)),
                  pl.BlockSpec((n_heads, LANES), lambda b, g, pt: (0, 0)),
                  pl.BlockSpec((page, n_heads, LANES), lambda b, g, pt: (0, 0, 0))]
                 + page_specs + page_specs,
        out_specs=tok_spec,
        scratch_shapes=[
            pltpu.VMEM((2, n_heads, LANES), F32),
            pltpu.VMEM((2, n_heads, LANES), F32),
            pltpu.VMEM((2, n_heads, kd), F32),
            pltpu.VMEM((page, 2, n_heads, LANES), F32),
            pltpu.VMEM((page, 2, n_heads, LANES), F32),
        ],
    )
    return pl.pallas_call(
        functools.partial(_decode_kernel, pages_per_step=pages_per_step, page=page,
                          n_pages=n_pages, lam_init=lam_init, tokens_per_iter=tokens_per_iter),
        grid_spec=grid_spec,
        out_shape=jax.ShapeDtypeStruct((nb, n_heads, kd), F32),
        compiler_params=_cparams(("parallel", "arbitrary")),
        name="sample_attention",
    )(page_table, q, k_new, v_new, *lams, sub_w, slope2, tok_bias,
      *([cache_k] * pages_per_step), *([cache_v] * pages_per_step))


def _merge_kernel(z_ref, halo_ref, bg_ref, o_ref, ga_ref, gb_ref, x_ref, cw_ref, wc_ref, wa_ref,
                  wo_ref, pw_ref, fw_ref, out_ref, h_ref, zbuf_ref, *, tm, tiles_per_seq):
    i = pl.program_id(0)
    z = z_ref[...]
    starts_seq = i % tiles_per_seq == 0
    zbuf_ref[0:SUBLANES, :] = jnp.where(starts_seq, 0.0, halo_ref[...])
    zbuf_ref[SUBLANES:, :] = z
    z1 = zbuf_ref[SUBLANES - 1:SUBLANES - 1 + tm, :]
    z2 = zbuf_ref[SUBLANES - 2:SUBLANES - 2 + tm, :]
    cw = cw_ref[...]
    zc = z2 * cw[0:1] + z1 * cw[1:2] + z * cw[2:3]
    y_conv = jnp.dot((bg_ref[...] * zc).astype(BF16), wc_ref[...], preferred_element_type=F32)
    y_attn = jnp.dot(o_ref[...], wa_ref[...], preferred_element_type=F32)
    merged = jax.nn.sigmoid(ga_ref[...]) * y_conv + jax.nn.sigmoid(gb_ref[...]) * y_attn
    mo = jnp.dot(merged.astype(BF16), wo_ref[...], preferred_element_type=F32)
    x_new = x_ref[...] + _rms(mo, pw_ref[...])
    out_ref[...] = x_new
    h_ref[...] = _rms(x_new, fw_ref[...]).astype(BF16)


def _merge(z, bg, o, gates, x, conv_w, wc, wa, wo, post_w, ffn_w, tm, seq, name):
    rows, d = x.shape
    dc = z.shape[1]
    row = lambda w: pl.BlockSpec((tm, w), lambda i: (i, 0))
    const = lambda a: pl.BlockSpec(a.shape, lambda i: (0, 0), pipeline_mode=pl.Buffered(1))
    hb = tm // SUBLANES
    halo = pl.BlockSpec((SUBLANES, dc), lambda i: (jnp.maximum(i * hb - 1, 0), 0))
    return pl.pallas_call(
        functools.partial(_merge_kernel, tm=tm, tiles_per_seq=seq // tm),
        grid=(rows // tm,),
        in_specs=[row(dc), halo, row(dc), row(d),
                  pl.BlockSpec((tm, d), lambda i: (i, 0)),
                  pl.BlockSpec((tm, d), lambda i: (i, 1)),
                  row(d), const(conv_w), const(wc), const(wa), const(wo), const(post_w),
                  const(ffn_w)],
        out_specs=[row(d), row(d)],
        out_shape=[jax.ShapeDtypeStruct((rows, d), F32), jax.ShapeDtypeStruct((rows, d), BF16)],
        scratch_shapes=[pltpu.VMEM((tm + SUBLANES, dc), F32)],
        compiler_params=_cparams(("arbitrary",)),
        name=name,
    )(z, z, bg, o, gates, gates, x, conv_w, wc, wa, wo, post_w, ffn_w)


def _ffn_kernel(x_ref, h_ref, wg_ref, wv_ref, cwg_ref, cwv_ref, wd_ref, pw_ref,
                out_ref, sg_ref, sv_ref, acc_ref, cg_ref, cv_ref, ug_ref, uv_ref,
                *, tm, tiles_per_seq):
    i = pl.program_id(0)
    j = pl.program_id(1)

    @pl.when(j == 0)
    def _():
        acc_ref[...] = jnp.zeros_like(acc_ref)

        @pl.when(i % tiles_per_seq == 0)
        def _():
            cg_ref[...] = jnp.zeros_like(cg_ref)
            cv_ref[...] = jnp.zeros_like(cv_ref)

    h = h_ref[...]
    tf = wd_ref.shape[0]

    def up_proj(cs, w_ref, carry_ref, ubuf_ref, state_ref):
        up = jnp.dot(h, w_ref[:, cs], preferred_element_type=F32)
        ubuf_ref[0:SUBLANES, cs] = carry_ref[j, :, cs]
        ubuf_ref[SUBLANES:, cs] = up
        carry_ref[j, :, cs] = up[tm - SUBLANES:, :]
        state_ref[:, cs] = up[tm - (CONV_W - 1):, :]

    def conv(cs, cw_ref, ubuf_ref):
        cw = cw_ref[:, cs]
        u0 = ubuf_ref[SUBLANES:SUBLANES + tm, cs]
        u1 = ubuf_ref[SUBLANES - 1:SUBLANES - 1 + tm, cs]
        u2 = ubuf_ref[SUBLANES - 2:SUBLANES - 2 + tm, cs]
        return u2 * cw[0:1] + u1 * cw[1:2] + u0 * cw[2:3]

    chunks = [slice(c0, c0 + FFN_SUB) for c0 in range(0, tf, FFN_SUB)]
    for cs in chunks:
        up_proj(cs, wg_ref, cg_ref, ug_ref, sg_ref)
        up_proj(cs, wv_ref, cv_ref, uv_ref, sv_ref)
    down = None
    for cs in chunks:
        gate = conv(cs, cwg_ref, ug_ref)
        val = conv(cs, cwv_ref, uv_ref)
        act = (gate * jax.nn.sigmoid(gate)) * val
        part = jnp.dot(act.astype(BF16), wd_ref[cs, :], preferred_element_type=F32)
        down = part if down is None else down + part
    acc_ref[...] += down

    @pl.when(j == pl.num_programs(1) - 1)
    def _():
        out_ref[...] = x_ref[...] + _rms(acc_ref[...], pw_ref[...])


def _ffn(x, h, wg, wv, conv_w, w_down, post_w, tm, seq, tf, name):
    rows, d = x.shape
    d_ff = w_down.shape[0]
    nj = d_ff // tf
    st_spec = pl.BlockSpec((None, CONV_W - 1, tf), lambda i, j: (i, 0, j))
    st_shape = jax.ShapeDtypeStruct((rows // tm, CONV_W - 1, d_ff), F32)
    return pl.pallas_call(
        functools.partial(_ffn_kernel, tm=tm, tiles_per_seq=seq // tm),
        grid=(rows // tm, nj),
        in_specs=[
            pl.BlockSpec((tm, d), lambda i, j: (i, 0)),
            pl.BlockSpec((tm, d), lambda i, j: (i, 0)),
            pl.BlockSpec((d, tf), lambda i, j: (0, j)),
            pl.BlockSpec((d, tf), lambda i, j: (0, j)),
            pl.BlockSpec((CONV_W, tf), lambda i, j: (0, j)),
            pl.BlockSpec((CONV_W, tf), lambda i, j: (0, nj + j)),
            pl.BlockSpec((tf, d), lambda i, j: (j, 0)),
            pl.BlockSpec((1, d), lambda i, j: (0, 0)),
        ],
        out_specs=[pl.BlockSpec((tm, d), lambda i, j: (i, 0)), st_spec, st_spec],
        out_shape=[jax.ShapeDtypeStruct((rows, d), F32), st_shape, st_shape],
        scratch_shapes=[
            pltpu.VMEM((tm, d), F32),
            pltpu.VMEM((nj, SUBLANES, tf), F32), pltpu.VMEM((nj, SUBLANES, tf), F32),
            pltpu.VMEM((tm + SUBLANES, tf), F32), pltpu.VMEM((tm + SUBLANES, tf), F32)],
        compiler_params=_cparams(("arbitrary", "arbitrary")),
        name=name,
    )(x, h, wg, wv, conv_w, conv_w, w_down, post_w)


def _s_proj_kernel(h_ref, w_ref, y_ref, wb_ref):
    wb = w_ref[...].astype(BF16)
    wb_ref[...] = wb
    y_ref[...] = jnp.dot(h_ref[...], wb, preferred_element_type=F32)


def _s_proj(h, w, col0, ncols, tn, name):
    nb, d = h.shape
    c0 = col0 // tn
    return pl.pallas_call(
        _s_proj_kernel,
        grid=(ncols // tn,),
        in_specs=[pl.BlockSpec((nb, d), lambda j: (0, 0)),
                  pl.BlockSpec((d, tn), lambda j: (0, c0 + j))],
        out_specs=[pl.BlockSpec((nb, tn), lambda j: (0, j)), pl.BlockSpec((d, tn), lambda j: (0, j))],
        out_shape=[jax.ShapeDtypeStruct((nb, ncols), F32), jax.ShapeDtypeStruct((d, ncols), BF16)],
        compiler_params=_cparams(("parallel",)),
        name=name,
    )(h, w)


def _s_conv_proj_kernel(h_ref, wu_ref, wb_ref, wc_ref, z_ref, bg_ref, wu_o, wb_o, wc_o):
    h = h_ref[...]
    wu = wu_ref[...].astype(BF16)
    wb = wb_ref[...].astype(BF16)
    wc = wc_ref[...].astype(BF16)
    wu_o[...] = wu
    wb_o[...] = wb
    wc_o[...] = wc
    u = jnp.dot(h, wu, preferred_element_type=F32)
    cg = jnp.dot(h, wc, preferred_element_type=F32)
    z_ref[...] = cg * u
    bg_ref[...] = jnp.dot(h, wb, preferred_element_type=F32)


def _s_conv_proj(h, w, d_conv, tn, name):
    nb, d = h.shape
    nt = d_conv // tn
    y_spec = pl.BlockSpec((nb, tn), lambda j: (0, j))
    w_out = pl.BlockSpec((d, tn), lambda j: (0, j))
    return pl.pallas_call(
        _s_conv_proj_kernel,
        grid=(nt,),
        in_specs=[pl.BlockSpec((nb, d), lambda j: (0, 0)),
                  pl.BlockSpec((d, tn), lambda j: (0, j)),
                  pl.BlockSpec((d, tn), lambda j: (0, nt + j)),
                  pl.BlockSpec((d, tn), lambda j: (0, 2 * nt + j))],
        out_specs=[y_spec, y_spec, w_out, w_out, w_out],
        out_shape=[jax.ShapeDtypeStruct((nb, d_conv), F32)] * 2
                  + [jax.ShapeDtypeStruct((d, d_conv), BF16)] * 3,
        compiler_params=_cparams(("parallel",)),
        name=name,
    )(h, w, w, w)


def _s_mix_kernel(z_ref, st0_ref, st1_ref, bg_ref, o_ref, ga_ref, gb_ref, cw_ref, wc_ref, wa_ref,
                  m_ref, wc_o, wa_o):
    cw = cw_ref[...]
    zc = st0_ref[...] * cw[0:1] + st1_ref[...] * cw[1:2] + z_ref[...] * cw[2:3]
    wc = wc_ref[...].astype(BF16)
    wa = wa_ref[...].astype(BF16)
    wc_o[...] = wc
    wa_o[...] = wa
    y_conv = jnp.dot((bg_ref[...] * zc).astype(BF16), wc, preferred_element_type=F32)
    y_attn = jnp.dot(o_ref[...], wa, preferred_element_type=F32)
    merged = jax.nn.sigmoid(ga_ref[...]) * y_conv + jax.nn.sigmoid(gb_ref[...]) * y_attn
    m_ref[...] = merged.astype(BF16)


def _s_mix(z, st0, st1, bg, o, gates, conv_w, w_conv_out, w_attn_out, tn, name):
    nb, dc = z.shape
    da, d = w_attn_out.shape
    nt = d // tn
    full = lambda a: pl.BlockSpec(a.shape, lambda j: (0, 0))
    return pl.pallas_call(
        _s_mix_kernel,
        grid=(nt,),
        in_specs=[full(z), full(st0), full(st1), full(bg), full(o),
                  pl.BlockSpec((nb, tn), lambda j: (0, j)),
                  pl.BlockSpec((nb, tn), lambda j: (0, nt + j)),
                  full(conv_w),
                  pl.BlockSpec((dc, tn), lambda j: (0, j)),
                  pl.BlockSpec((da, tn), lambda j: (0, j))],
        out_specs=[pl.BlockSpec((nb, tn), lambda j: (0, j)),
                   pl.BlockSpec((dc, tn), lambda j: (0, j)),
                   pl.BlockSpec((da, tn), lambda j: (0, j))],
        out_shape=[jax.ShapeDtypeStruct((nb, d), BF16),
                   jax.ShapeDtypeStruct(w_conv_out.shape, BF16),
                   jax.ShapeDtypeStruct(w_attn_out.shape, BF16)],
        compiler_params=_cparams(("parallel",)),
        name=name,
    )(z, st0, st1, bg, o, gates, gates, conv_w, w_conv_out, w_attn_out)


def _s_out_kernel(m_ref, wo_ref, x_ref, pw_ref, fw_ref, out_ref, h_ref, wo_o, acc_ref):
    k = pl.program_id(0)

    @pl.when(k == 0)
    def _():
        acc_ref[...] = jnp.zeros_like(acc_ref)

    wo = wo_ref[...].astype(BF16)
    wo_o[...] = wo
    acc_ref[...] += jnp.dot(m_ref[...], wo, preferred_element_type=F32)

    @pl.when(k == pl.num_programs(0) - 1)
    def _():
        x_new = x_ref[...] + _rms(acc_ref[...], pw_ref[...])
        out_ref[...] = x_new
        h_ref[...] = _rms(x_new, fw_ref[...]).astype(BF16)


def _s_out(merged, w_o, x, post_w, ffn_w, tk, name):
    nb, d = x.shape
    full = lambda a: pl.BlockSpec(a.shape, lambda k: (0, 0))
    return pl.pallas_call(
        _s_out_kernel,
        grid=(d // tk,),
        in_specs=[pl.BlockSpec((nb, tk), lambda k: (0, k)),
                  pl.BlockSpec((tk, d), lambda k: (k, 0)),
                  full(x), full(post_w), full(ffn_w)],
        out_specs=[full(x), full(x), pl.BlockSpec((tk, d), lambda k: (k, 0))],
        out_shape=[jax.ShapeDtypeStruct((nb, d), F32), jax.ShapeDtypeStruct((nb, d), BF16),
                   jax.ShapeDtypeStruct(w_o.shape, BF16)],
        scratch_shapes=[pltpu.VMEM((nb, d), F32)],
        compiler_params=_cparams(("arbitrary",)),
        name=name,
    )(merged, w_o, x, post_w, ffn_w)


def _s_ffn_kernel(x_ref, h_ref, wg_ref, wv_ref, cwg_ref, cwv_ref, wd_ref, pw_ref,
                  g0_ref, g1_ref, v0_ref, v1_ref,
                  out_ref, ug_ref, uv_ref, wg_o, wv_o, wd_o, acc_ref):
    j = pl.program_id(0)

    @pl.when(j == 0)
    def _():
        acc_ref[...] = jnp.zeros_like(acc_ref)

    h = h_ref[...]
    wg = wg_ref[...].astype(BF16)
    wv = wv_ref[...].astype(BF16)
    wd = wd_ref[...].astype(BF16)
    wg_o[...] = wg
    wv_o[...] = wv
    wd_o[...] = wd
    up_g = jnp.dot(h, wg, preferred_element_type=F32)
    up_v = jnp.dot(h, wv, preferred_element_type=F32)
    ug_ref[...] = up_g
    uv_ref[...] = up_v
    cwg = cwg_ref[...]
    cwv = cwv_ref[...]
    gate = g0_ref[...] * cwg[0:1] + g1_ref[...] * cwg[1:2] + up_g * cwg[2:3]
    val = v0_ref[...] * cwv[0:1] + v1_ref[...] * cwv[1:2] + up_v * cwv[2:3]
    act = (gate * jax.nn.sigmoid(gate)) * val
    acc_ref[...] += jnp.dot(act.astype(BF16), wd, preferred_element_type=F32)

    @pl.when(j == pl.num_programs(0) - 1)
    def _():
        out_ref[...] = x_ref[...] + _rms(acc_ref[...], pw_ref[...])


def _s_ffn(x, h, w_up, conv_w, w_down, post_w, prev, tf, name):
    nb, d = x.shape
    d_ff = w_down.shape[0]
    nj = d_ff // tf
    full = lambda a: pl.BlockSpec(a.shape, lambda j: (0, 0))
    col = pl.BlockSpec((nb, tf), lambda j: (0, j))
    up_w = pl.BlockSpec((d, tf), lambda j: (0, j))
    down_w = pl.BlockSpec((tf, d), lambda j: (j, 0))
    return pl.pallas_call(
        _s_ffn_kernel,
        grid=(nj,),
        in_specs=[full(x), full(h), up_w, pl.BlockSpec((d, tf), lambda j: (0, nj + j)),
                  pl.BlockSpec((CONV_W, tf), lambda j: (0, j)),
                  pl.BlockSpec((CONV_W, tf), lambda j: (0, nj + j)),
                  down_w, full(post_w), col, col, col, col],
        out_specs=[full(x), col, col, up_w, up_w, down_w],
        out_shape=[jax.ShapeDtypeStruct((nb, d), F32),
                   jax.ShapeDtypeStruct((nb, d_ff), F32), jax.ShapeDtypeStruct((nb, d_ff), F32),
                   jax.ShapeDtypeStruct((d, d_ff), BF16), jax.ShapeDtypeStruct((d, d_ff), BF16),
                   jax.ShapeDtypeStruct((d_ff, d), BF16)],
        scratch_shapes=[pltpu.VMEM((nb, d), F32)],
        compiler_params=_cparams(("arbitrary",)),
        name=name,
    )(x, h, w_up, w_up, conv_w, conv_w, w_down, post_w, *prev)


def kernel(x_prompt, x_sample, cache_k, cache_v, page_table, state_conv, state_ffn_conv,
           pre_mix_w, w_in, conv_w, w_conv_out, lambda_q1, lambda_k1, lambda_q2, lambda_k2,
           subln_w, w_attn_out, w_o, post_mix_w, pre_ffn_w, w_up, ffn_conv_w, w_down, post_ffn_w):
    batch, seq, d = x_prompt.shape
    nb = x_sample.shape[0]
    depth = w_in.shape[0]
    n_heads = cache_k.shape[3]
    kd = cache_k.shape[4]
    d_conv = conv_w.shape[2]
    d_ff = w_down.shape[1]
    rows = batch * seq
    qkv = n_heads * kd
    c_q, c_k, c_v, c_g = 3 * d_conv, 3 * d_conv + qkv, 3 * d_conv + 2 * qkv, 3 * d_conv + 3 * qkv
    slopes = jnp.exp2(-8.0 * jnp.arange(1, n_heads + 1, dtype=F32) / n_heads)

    xp = x_prompt.reshape(rows, d)
    xs = x_sample.reshape(nb, d)
    outs = {n: [] for n in ("kp", "vp", "cp", "fp", "ks", "vs", "cs", "fs")}
    tm, tn = 1024, 1024
    tm_merge, tm_ffn, tf = 256, 512, 512
    ts = 512

    for l in range(depth):
        lam_init = 0.8 - 0.6 * math.exp(-0.3 * l)
        row2 = lambda a: a[l].reshape(1, -1)
        lams = [row2(lambda_q1), row2(lambda_k1), row2(lambda_q2), row2(lambda_k2)]
        pre_w, post_w = row2(pre_mix_w), row2(post_mix_w)
        pre_f, post_f, sub_w = row2(pre_ffn_w), row2(post_ffn_w), row2(subln_w)

        hs = _norm(xs, pre_w, nb)
        zs, bgs, wu_b, wb_b, wcg_b = _s_conv_proj(hs, w_in[l], d_conv, ts, "s_conv_proj")
        qkv_s, wqkv_b = _s_proj(hs, w_in[l], c_q, 3 * qkv, ts, "s_qkv_proj")
        gates_s, wg_b = _s_proj(hs, w_in[l], c_g, 2 * d, ts, "s_gate_proj")
        q_s = qkv_s[:, :qkv].reshape(nb, n_heads, kd)
        k_s = qkv_s[:, qkv:2 * qkv].reshape(nb, n_heads, kd)
        v_s = qkv_s[:, 2 * qkv:].reshape(nb, n_heads, kd)
        o_s = _decode_attention(q_s, k_s, v_s, cache_k, cache_v, page_table, l, slopes, lams,
                                sub_w, lam_init)
        sc = state_conv[l]
        merged_s, wc_b, wa_b = _s_mix(zs, sc[:, 0], sc[:, 1], bgs, o_s.reshape(nb, qkv).astype(BF16),
                                      gates_s, conv_w[l], w_conv_out[l], w_attn_out[l], ts, "s_mix")
        xs, hfs, wo_b = _s_out(merged_s, w_o[l], xs, post_w, pre_f, ts, "s_out")
        sf = state_ffn_conv[l]
        prev = (sf[:, 0, :d_ff], sf[:, 1, :d_ff], sf[:, 0, d_ff:], sf[:, 1, d_ff:])
        xs, ug, uv, wug_b, wuv_b, wd_b = _s_ffn(xs, hfs, w_up[l], ffn_conv_w[l], w_down[l], post_f,
                                                prev, tf, "s_ffn")
        outs["ks"].append(k_s.reshape(nb, 1, n_heads, kd))
        outs["vs"].append(v_s.reshape(nb, 1, n_heads, kd))
        outs["cs"].append(jnp.stack([sc[:, 1], zs], axis=1))
        outs["fs"].append(jnp.stack([sf[:, 1], jnp.concatenate([ug, uv], axis=-1)], axis=1))

        hp = _norm(xp, pre_w, 512)
        z, bg = _conv_proj(hp, wu_b, wb_b, wcg_b, tm, 512, "p_conv_proj")
        (q_b,) = _proj(hp, wqkv_b, 0, qkv, "bf16", tm, tn, "p_q_proj")
        k_f, k_b = _proj(hp, wqkv_b, qkv, qkv, "both", tm, tn, "p_k_proj")
        v_f, v_b = _proj(hp, wqkv_b, 2 * qkv, qkv, "both", tm, tn, "p_v_proj")
        (gates,) = _proj(hp, wg_b, 0, 2 * d, "f32", tm, tn, "p_gate_proj")
        o = _flash_attention(q_b, k_b, v_b, slopes, lams, sub_w, batch, seq, n_heads, lam_init)
        xp, hf = _merge(z, bg, o, gates, xp, conv_w[l], wc_b, wa_b, wo_b, post_w, pre_f,
                        tm_merge, seq, "p_merge")
        xp, fg, fv = _ffn(xp, hf, wug_b, wuv_b, ffn_conv_w[l], wd_b, post_f, tm_ffn, seq, tf, "p_ffn")
        last = seq // tm_ffn - 1
        outs["kp"].append(k_f.reshape(batch, seq, n_heads, kd))
        outs["vp"].append(v_f.reshape(batch, seq, n_heads, kd))
        outs["cp"].append(z.reshape(batch, seq, d_conv)[:, seq - (CONV_W - 1):])
        outs["fp"].append(jnp.concatenate([fg, fv], axis=-1)[last::seq // tm_ffn])

    st = lambda n: jnp.stack(outs[n])
    return (xp.reshape(batch, seq, d), xs.reshape(nb, 1, d),
            st("kp"), st("vp"), st("cp"), st("fp"),
            st("ks"), st("vs"), st("cs"), st("fs"))
```

```python
import functools
import math

import jax
import jax.numpy as jnp
import numpy as np
from jax import lax
from jax.experimental import pallas as pl
from jax.experimental.pallas import tpu as pltpu

F32 = jnp.float32
BF16 = jnp.bfloat16

HEAD_DIM = 128
CONV_W = 3
RMS_EPS = 1e-6
NEG_BIG = -1e30
LOG2E = 1.4426950408889634
LANES = 128
SUBLANES = 8
VMEM_LIMIT = 56 * 1024 * 1024
PARTIAL_SUMS = 1
FFN_SUB = 256


def _bf16_pieces(x, n):
    pieces = []
    for _ in range(n):
        p = float(np.asarray(x, np.float32).astype(BF16))
        pieces.append(p)
        x -= p
    return tuple(pieces)


SQRT_HEAD_DIM_PIECES = _bf16_pieces(HEAD_DIM ** 0.5, 3)


def _cparams(sem):
    return pltpu.CompilerParams(dimension_semantics=sem, vmem_limit_bytes=VMEM_LIMIT)


def _rms(x, w):
    ms = jnp.mean(x * x, axis=-1, keepdims=True)
    return x * lax.rsqrt(ms + RMS_EPS) * w


def _norm_kernel(x_ref, nw_ref, h_ref):
    h_ref[...] = _rms(x_ref[...], nw_ref[...]).astype(BF16)


def _norm(x, nw, tm):
    rows, d = x.shape
    return pl.pallas_call(
        _norm_kernel,
        grid=(rows // tm,),
        in_specs=[pl.BlockSpec((tm, d), lambda i: (i, 0)), pl.BlockSpec((1, d), lambda i: (0, 0))],
        out_specs=pl.BlockSpec((tm, d), lambda i: (i, 0)),
        out_shape=jax.ShapeDtypeStruct((rows, d), BF16),
        compiler_params=_cparams(("parallel",)),
        name="rmsnorm",
    )(x, nw)


def _proj_kernel(h_ref, w_ref, *o_refs, outs):
    y = jnp.dot(h_ref[...], w_ref[...], preferred_element_type=F32)
    if outs == "f32":
        o_refs[0][...] = y
    elif outs == "bf16":
        o_refs[0][...] = y.astype(BF16)
    else:
        o_refs[0][...] = y
        o_refs[1][...] = y.astype(BF16)


def _proj(h, w, col0, ncols, outs, tm, tn, name):
    rows, d = h.shape
    c0 = col0 // tn
    dts = {"f32": [F32], "bf16": [BF16], "both": [F32, BF16]}[outs]
    return pl.pallas_call(
        functools.partial(_proj_kernel, outs=outs),
        grid=(rows // tm, ncols // tn),
        in_specs=[
            pl.BlockSpec((tm, d), lambda i, j: (i, 0)),
            pl.BlockSpec((d, tn), lambda i, j: (0, c0 + j)),
        ],
        out_specs=[pl.BlockSpec((tm, tn), lambda i, j: (i, j)) for _ in dts],
        out_shape=[jax.ShapeDtypeStruct((rows, ncols), dt) for dt in dts],
        compiler_params=_cparams(("parallel", "parallel")),
        name=name,
    )(h, w)


def _conv_proj_kernel(h_ref, wu_ref, wb_ref, wc_ref, z_ref, bg_ref):
    h = h_ref[...]
    u = jnp.dot(h, wu_ref[...], preferred_element_type=F32)
    cg = jnp.dot(h, wc_ref[...], preferred_element_type=F32)
    z_ref[...] = cg * u
    bg_ref[...] = jnp.dot(h, wb_ref[...], preferred_element_type=F32)


def _conv_proj(h, wu, wb, wc, tm, tn, name):
    rows, d = h.shape
    d_conv = wu.shape[1]
    w_spec = pl.BlockSpec((d, tn), lambda i, j: (0, j))
    return pl.pallas_call(
        _conv_proj_kernel,
        grid=(rows // tm, d_conv // tn),
        in_specs=[pl.BlockSpec((tm, d), lambda i, j: (i, 0)), w_spec, w_spec, w_spec],
        out_specs=[pl.BlockSpec((tm, tn), lambda i, j: (i, j))] * 2,
        out_shape=[jax.ShapeDtypeStruct((rows, d_conv), F32)] * 2,
        compiler_params=_cparams(("parallel", "parallel")),
        name=name,
    )(h, wu, wb, wc)


def _lambda_value(lq1_ref, lk1_ref, lq2_ref, lk2_ref, lam_init):
    a = jnp.sum(lq1_ref[...] * lk1_ref[...], axis=-1, keepdims=True)
    b = jnp.sum(lq2_ref[...] * lk2_ref[...], axis=-1, keepdims=True)
    return jnp.exp(a) - jnp.exp(b) + lam_init


def _head_out(o, sub_w, lam_init):
    return _rms(o, sub_w) * (1.0 - lam_init)


def _flash_kernel(slope_ref, q_ref, k_ref, v_ref, lq1_ref, lk1_ref, lq2_ref, lk2_ref,
                  subt_ref, o_ref, acc1_ref, acc2_ref, vt_ref, sa1_ref, sa2_ref, sb1_ref, sb2_ref,
                  *, tq, lam_init):
    h = pl.program_id(1)
    qi = pl.program_id(2)
    slope = slope_ref[h]
    scale2 = HEAD_DIM ** -0.5 * LOG2E

    lane = lax.broadcasted_iota(jnp.int32, (tq, HEAD_DIM), 1)
    pos = lax.broadcasted_iota(jnp.int32, (tq, HEAD_DIM), 0)
    n_pc = len(SQRT_HEAD_DIM_PIECES)
    low = pos & 1
    k_feat = jnp.where(lane < n_pc, pos - low, jnp.where(lane < 2 * n_pc, low, 0))
    aug_k = (k_feat.astype(F32) * slope).astype(BF16)
    q_feat = jnp.zeros((tq, HEAD_DIM), F32)
    for n, piece in enumerate(SQRT_HEAD_DIM_PIECES):
        q_feat = jnp.where((lane == n) | (lane == n + n_pc), piece, q_feat)
    aug_q = q_feat.astype(BF16)
    qa = [jnp.concatenate([q_ref[:, c0:c0 + HEAD_DIM], aug_q], axis=1) for c0 in (0, HEAD_DIM)]

    tk = tq // 2
    slope2 = slope * LOG2E

    @pl.when(qi == 0)
    def _():
        for blk in range(vt_ref.shape[0]):
            vt_ref[blk] = v_ref[blk * tk:(blk + 1) * tk, :].astype(F32).T.astype(BF16)

    acc_refs = (acc1_ref, acc2_ref)
    acc1_ref[...] = jnp.zeros_like(acc1_ref)
    acc2_ref[...] = jnp.zeros_like(acc2_ref)

    def qk_store(kb, dst_refs):
        off = pl.multiple_of(kb * tk, tk)
        k = k_ref[pl.ds(off, tk), :]
        for n in range(2):
            ka = jnp.concatenate([k[:, n * HEAD_DIM:(n + 1) * HEAD_DIM], aug_k[:tk]], axis=1)
            s = lax.dot_general(ka, qa[n], (((1,), (1,)), ((), ())), preferred_element_type=F32)
            dst_refs[n][...] = s * scale2

    def consume(kb, src_refs, carry, mask_shift):
        vt = vt_ref[kb]
        c = slope2 * (kb * tk - qi * tq).astype(F32)
        new_carry, alphas, ps = [], [], []
        for n in range(2):
            m, l = carry[2 * n], carry[2 * n + 1]
            s = src_refs[n][...]
            if mask_shift is not None:
                key = lax.broadcasted_iota(jnp.int32, (tk, tq), 0)
                qry = lax.broadcasted_iota(jnp.int32, (tk, tq), 1)
                s = jnp.where(key + mask_shift <= qry, s, NEG_BIG)
            m_new = jnp.maximum(m, jnp.max(s, axis=0, keepdims=True) + c)
            alpha = jnp.exp2(m - m_new)
            p = jnp.exp2(s - (m_new - c))
            new_carry += [m_new, alpha * l + jnp.sum(p, axis=0, keepdims=True)]
            alphas.append(alpha)
            ps.append(p.astype(BF16))
        for n in range(2):
            acc_refs[n][...] = alphas[n] * acc_refs[n][...] + jnp.dot(
                vt, ps[n], preferred_element_type=F32)
        return tuple(new_carry)

    buf_a = (sa1_ref, sa2_ref)
    buf_b = (sb1_ref, sb2_ref)
    qk_store(0, buf_a)

    def pair(t, carry):
        kb = 2 * t
        qk_store(kb + 1, buf_b)
        carry = consume(kb, buf_a, carry, None)
        qk_store(kb + 2, buf_a)
        return consume(kb + 1, buf_b, carry, None)

    m0 = jnp.full((1, tq), NEG_BIG, F32)
    l0 = jnp.zeros((1, tq), F32)
    carry = lax.fori_loop(0, qi, pair, (m0, l0, m0, l0))
    qk_store(2 * qi + 1, buf_b)
    carry = consume(2 * qi, buf_a, carry, 0)
    m1, l1, m2, l2 = consume(2 * qi + 1, buf_b, carry, tk)

    lam = _lambda_value(lq1_ref, lk1_ref, lq2_ref, lk2_ref, lam_init)
    o_t = acc1_ref[...] / l1 - lam * (acc2_ref[...] / l2)
    ms = jnp.mean(o_t * o_t, axis=0, keepdims=True)
    y_t = o_t * lax.rsqrt(ms + RMS_EPS) * subt_ref[...] * (1.0 - lam_init)
    o_ref[...] = y_t.T.astype(BF16)


def _flash_attention(q, k, v, slopes, lams, sub_w, batch, seq, n_heads, lam_init, tq=512):
    kd = 2 * HEAD_DIM
    nq = seq // tq
    lam_spec = pl.BlockSpec((1, HEAD_DIM), lambda b, h, i: (0, 0))
    return pl.pallas_call(
        functools.partial(_flash_kernel, tq=tq, lam_init=lam_init),
        grid=(batch, n_heads, nq),
        in_specs=[
            pl.BlockSpec(memory_space=pltpu.SMEM),
            pl.BlockSpec((tq, kd), lambda b, h, i: (b * nq + i, h)),
            pl.BlockSpec((seq, kd), lambda b, h, i: (b, h)),
            pl.BlockSpec((seq, kd), lambda b, h, i: (b, h)),
            lam_spec, lam_spec, lam_spec, lam_spec,
            pl.BlockSpec((kd, 1), lambda b, h, i: (0, 0)),
        ],
        out_specs=pl.BlockSpec((tq, kd), lambda b, h, i: (b * nq + i, h)),
        out_shape=jax.ShapeDtypeStruct(q.shape, BF16),
        scratch_shapes=[pltpu.VMEM((kd, tq), F32), pltpu.VMEM((kd, tq), F32),
                        pltpu.VMEM((2 * nq, kd, tq // 2), BF16)]
                       + [pltpu.VMEM((tq // 2, tq), F32)] * 4,
        compiler_params=_cparams(("parallel", "parallel", "arbitrary")),
        name="prompt_attention",
    )(slopes, q, k, v, *lams, sub_w.reshape(kd, 1))


def _decode_kernel(pt_ref, q_ref, kn_ref, vn_ref, lq1_ref, lk1_ref, lq2_ref, lk2_ref,
                   sub_ref, slope_ref, tbl_ref, *rest, pages_per_step, page, n_pages,
                   lam_init, tokens_per_iter):
    del pt_ref
    k_refs = rest[:pages_per_step]
    v_refs = rest[pages_per_step:2 * pages_per_step]
    o_ref, m_ref, l_ref, acc_ref, s_even_ref, s_odd_ref = rest[2 * pages_per_step:]
    s_refs = (s_even_ref, s_odd_ref)
    g = pl.program_id(1)
    past = n_pages * page
    n_iter = page // tokens_per_iter

    @pl.when(g == 0)
    def _():
        m_ref[...] = jnp.full_like(m_ref, NEG_BIG)
        l_ref[...] = jnp.zeros_like(l_ref)
        acc_ref[...] = jnp.zeros_like(acc_ref)

    qv = q_ref[0] * (HEAD_DIM ** -0.5 * LOG2E)
    q1 = qv[:, :HEAD_DIM]
    q2 = qv[:, HEAD_DIM:]
    slope2 = slope_ref[...]

    def rowsum(x):
        return jnp.broadcast_to(jnp.sum(x, axis=-1, keepdims=True), x.shape)

    def twice(p):
        return jnp.concatenate([p, p], axis=-1)

    def run(score_page, acc_page, shifts):
        if score_page is not None:
            k_ref, sa_ref = k_refs[score_page], s_refs[score_page % 2]
        if acc_page is not None:
            v_ref, sb_ref = v_refs[acc_page], s_refs[acc_page % 2]

        def body(it, carry):
            mx1, mx2, parts = carry
            parts = [list(p) for p in parts]
            base = it * tokens_per_iter
            for tt in range(tokens_per_iter):
                t = base + tt
                if score_page is not None:
                    kt = k_ref[t]
                    tb = tbl_ref[t]
                    u1 = rowsum(kt[:, :HEAD_DIM] * q1) + tb
                    u2 = rowsum(kt[:, HEAD_DIM:] * q2) + tb
                    sa_ref[t, 0] = u1
                    sa_ref[t, 1] = u2
                    mx1 = jnp.maximum(mx1, u1)
                    mx2 = jnp.maximum(mx2, u2)
                if acc_page is not None:
                    p1 = jnp.exp2(sb_ref[t, 0] - shifts[0])
                    p2 = jnp.exp2(sb_ref[t, 1] - shifts[1])
                    vt = v_ref[t]
                    part = parts[tt % PARTIAL_SUMS]
                    part[0] = part[0] + p1
                    part[1] = part[1] + p2
                    part[2] = part[2] + twice(p1) * vt
                    part[3] = part[3] + twice(p2) * vt
            return mx1, mx2, tuple(tuple(p) for p in parts)

        neg = jnp.full((SUBLANES, LANES), NEG_BIG, F32)
        zero = (jnp.zeros_like(l_ref[0]), jnp.zeros_like(l_ref[0]),
                jnp.zeros_like(acc_ref[0]), jnp.zeros_like(acc_ref[0]))
        first = (l_ref[0], l_ref[1], acc_ref[0], acc_ref[1])
        if acc_page is None:
            parts0 = ()
        else:
            parts0 = (first,) + (zero,) * (PARTIAL_SUMS - 1)
        mx1, mx2, parts = lax.fori_loop(0, n_iter, body, (neg, neg, parts0))
        if acc_page is not None:
            l_ref[0] = sum(p[0] for p in parts[1:]) + parts[0][0]
            l_ref[1] = sum(p[1] for p in parts[1:]) + parts[0][1]
            acc_ref[0] = sum(p[2] for p in parts[1:]) + parts[0][2]
            acc_ref[1] = sum(p[3] for p in parts[1:]) + parts[0][3]
        return mx1, mx2

    def new_max(c, m_cand):
        m_new = jnp.maximum(m_ref[c], m_cand)
        alpha = jnp.exp2(m_ref[c] - m_new)
        m_ref[c] = m_new
        l_ref[c] = alpha * l_ref[c]
        acc_ref[c] = twice(alpha) * acc_ref[c]
        return m_new

    def page_shifts(r, mx1, mx2):
        first_pos = (g * pages_per_step + r) * page
        c = slope2 * (first_pos - past).astype(F32)
        return new_max(0, mx1 + c) - c, new_max(1, mx2 + c) - c

    mx = run(0, None, None)
    for r in range(pages_per_step):
        shifts = page_shifts(r, *mx)
        if r + 1 < pages_per_step:
            mx = run(r + 1, r, shifts)
        else:
            run(None, r, shifts)

    @pl.when(g == pl.num_programs(1) - 1)
    def _():
        kn = kn_ref[0]
        vn = vn_ref[0]
        s1 = rowsum(kn[:, :HEAD_DIM] * q1)
        s2 = rowsum(kn[:, HEAD_DIM:] * q2)
        p1 = jnp.exp2(s1 - new_max(0, s1))
        p2 = jnp.exp2(s2 - new_max(1, s2))
        o1 = (acc_ref[0] + twice(p1) * vn) / twice(l_ref[0] + p1)
        o2 = (acc_ref[1] + twice(p2) * vn) / twice(l_ref[1] + p2)
        lam = _lambda_value(lq1_ref, lk1_ref, lq2_ref, lk2_ref, lam_init)
        o_ref[0] = _head_out(o1 - lam * o2, sub_ref[...], lam_init)


def _decode_attention(q, k_new, v_new, cache_k, cache_v, page_table, layer, slopes, lams,
                      sub_w, lam_init, pages_per_step=8, tokens_per_iter=128):
    nb, n_heads, kd = q.shape
    n_pages = page_table.shape[1]
    page = cache_k.shape[2]
    steps = n_pages // pages_per_step
    slope2 = jnp.broadcast_to((slopes * LOG2E)[:, None], (n_heads, LANES))
    tok_bias = jnp.arange(page, dtype=F32)[:, None, None] * slope2[None]

    tok_spec = pl.BlockSpec((1, n_heads, kd), lambda b, g, pt: (b, 0, 0))
    lam_spec = pl.BlockSpec((1, HEAD_DIM), lambda b, g, pt: (0, 0))

    def page_spec(r):
        return pl.BlockSpec(
            (None, None, page, n_heads, kd),
            lambda b, g, pt: (layer, pt[b, g * pages_per_step + r], 0, 0, 0))

    page_specs = [page_spec(r) for r in range(pages_per_step)]
    grid_spec = pltpu.PrefetchScalarGridSpec(
        num_scalar_prefetch=1,
        grid=(nb, steps),
        in_specs=[tok_spec, tok_spec, tok_spec, lam_spec, lam_spec, lam_spec, lam_spec,
                  pl.BlockSpec((1, kd), lambda b, g, pt: (0, 0)),
                  pl.BlockSpec((n_heads, LANES), lambda b, g, pt: (0, 0)),
                  pl.BlockSpec((page, n_heads, LANES), lambda b, g, pt: (0, 0, 0))]
                 + page_specs + page_specs,
        out_specs=tok_spec,
        scratch_shapes=[
            pltpu.VMEM((2, n_heads, LANES), F32),
            pltpu.VMEM((2, n_heads, LANES), F32),
            pltpu.VMEM((2, n_heads, kd), F32),
            pltpu.VMEM((page, 2, n_heads, LANES), F32),
            pltpu.VMEM((page, 2, n_heads, LANES), F32),
        ],
    )
    return pl.pallas_call(
        functools.partial(_decode_kernel, pages_per_step=pages_per_step, page=page,
                          n_pages=n_pages, lam_init=lam_init, tokens_per_iter=tokens_per_iter),
        grid_spec=grid_spec,
        out_shape=jax.ShapeDtypeStruct((nb, n_heads, kd), F32),
        compiler_params=_cparams(("parallel", "arbitrary")),
        name="sample_attention",
    )(page_table, q, k_new, v_new, *lams, sub_w, slope2, tok_bias,
      *([cache_k] * pages_per_step), *([cache_v] * pages_per_step))


def _merge_kernel(z_ref, halo_ref, bg_ref, o_ref, ga_ref, gb_ref, x_ref, cw_ref, wc_ref, wa_ref,
                  wo_ref, pw_ref, fw_ref, out_ref, h_ref, zbuf_ref, *, tm, tiles_per_seq):
    i = pl.program_id(0)
    z = z_ref[...]
    starts_seq = i % tiles_per_seq == 0
    zbuf_ref[0:SUBLANES, :] = jnp.where(starts_seq, 0.0, halo_ref[...])
    zbuf_ref[SUBLANES:, :] = z
    z1 = zbuf_ref[SUBLANES - 1:SUBLANES - 1 + tm, :]
    z2 = zbuf_ref[SUBLANES - 2:SUBLANES - 2 + tm, :]
    cw = cw_ref[...]
    zc = z2 * cw[0:1] + z1 * cw[1:2] + z * cw[2:3]
    y_conv = jnp.dot((bg_ref[...] * zc).astype(BF16), wc_ref[...], preferred_element_type=F32)
    y_attn = jnp.dot(o_ref[...], wa_ref[...], preferred_element_type=F32)
    merged = jax.nn.sigmoid(ga_ref[...]) * y_conv + jax.nn.sigmoid(gb_ref[...]) * y_attn
    mo = jnp.dot(merged.astype(BF16), wo_ref[...], preferred_element_type=F32)
    x_new = x_ref[...] + _rms(mo, pw_ref[...])
    out_ref[...] = x_new
    h_ref[...] = _rms(x_new, fw_ref[...]).astype(BF16)


def _merge(z, bg, o, gates, x, conv_w, wc, wa, wo, post_w, ffn_w, tm, seq, name):
    rows, d = x.shape
    dc = z.shape[1]
    row = lambda w: pl.BlockSpec((tm, w), lambda i: (i, 0))
    const = lambda a: pl.BlockSpec(a.shape, lambda i: (0, 0), pipeline_mode=pl.Buffered(1))
    hb = tm // SUBLANES
    halo = pl.BlockSpec((SUBLANES, dc), lambda i: (jnp.maximum(i * hb - 1, 0), 0))
    return pl.pallas_call(
        functools.partial(_merge_kernel, tm=tm, tiles_per_seq=seq // tm),
        grid=(rows // tm,),
        in_specs=[row(dc), halo, row(dc), row(d),
                  pl.BlockSpec((tm, d), lambda i: (i, 0)),
                  pl.BlockSpec((tm, d), lambda i: (i, 1)),
                  row(d), const(conv_w), const(wc), const(wa), const(wo), const(post_w),
                  const(ffn_w)],
        out_specs=[row(d), row(d)],
        out_shape=[jax.ShapeDtypeStruct((rows, d), F32), jax.ShapeDtypeStruct((rows, d), BF16)],
        scratch_shapes=[pltpu.VMEM((tm + SUBLANES, dc), F32)],
        compiler_params=_cparams(("arbitrary",)),
        name=name,
    )(z, z, bg, o, gates, gates, x, conv_w, wc, wa, wo, post_w, ffn_w)


def _ffn_kernel(x_ref, h_ref, wg_ref, wv_ref, cwg_ref, cwv_ref, wd_ref, pw_ref,
                out_ref, sg_ref, sv_ref, acc_ref, cg_ref, cv_ref, ug_ref, uv_ref,
                *, tm, tiles_per_seq):
    i = pl.program_id(0)
    j = pl.program_id(1)

    @pl.when(j == 0)
    def _():
        acc_ref[...] = jnp.zeros_like(acc_ref)

        @pl.when(i % tiles_per_seq == 0)
        def _():
            cg_ref[...] = jnp.zeros_like(cg_ref)
            cv_ref[...] = jnp.zeros_like(cv_ref)

    h = h_ref[...]
    tf = wd_ref.shape[0]

    def up_proj(cs, w_ref, carry_ref, ubuf_ref, state_ref):
        up = jnp.dot(h, w_ref[:, cs], preferred_element_type=F32)
        ubuf_ref[0:SUBLANES, cs] = carry_ref[j, :, cs]
        ubuf_ref[SUBLANES:, cs] = up
        carry_ref[j, :, cs] = up[tm - SUBLANES:, :]
        state_ref[:, cs] = up[tm - (CONV_W - 1):, :]

    def conv(cs, cw_ref, ubuf_ref):
        cw = cw_ref[:, cs]
        u0 = ubuf_ref[SUBLANES:SUBLANES + tm, cs]
        u1 = ubuf_ref[SUBLANES - 1:SUBLANES - 1 + tm, cs]
        u2 = ubuf_ref[SUBLANES - 2:SUBLANES - 2 + tm, cs]
        return u2 * cw[0:1] + u1 * cw[1:2] + u0 * cw[2:3]

    chunks = [slice(c0, c0 + FFN_SUB) for c0 in range(0, tf, FFN_SUB)]
    for cs in chunks:
        up_proj(cs, wg_ref, cg_ref, ug_ref, sg_ref)
        up_proj(cs, wv_ref, cv_ref, uv_ref, sv_ref)
    down = None
    for cs in chunks:
        gate = conv(cs, cwg_ref, ug_ref)
        val = conv(cs, cwv_ref, uv_ref)
        act = (gate * jax.nn.sigmoid(gate)) * val
        part = jnp.dot(act.astype(BF16), wd_ref[cs, :], preferred_element_type=F32)
        down = part if down is None else down + part
    acc_ref[...] += down

    @pl.when(j == pl.num_programs(1) - 1)
    def _():
        out_ref[...] = x_ref[...] + _rms(acc_ref[...], pw_ref[...])


def _ffn(x, h, wg, wv, conv_w, w_down, post_w, tm, seq, tf, name):
    rows, d = x.shape
    d_ff = w_down.shape[0]
    nj = d_ff // tf
    st_spec = pl.BlockSpec((None, CONV_W - 1, tf), lambda i, j: (i, 0, j))
    st_shape = jax.ShapeDtypeStruct((rows // tm, CONV_W - 1, d_ff), F32)
    return pl.pallas_call(
        functools.partial(_ffn_kernel, tm=tm, tiles_per_seq=seq // tm),
        grid=(rows // tm, nj),
        in_specs=[
            pl.BlockSpec((tm, d), lambda i, j: (i, 0)),
            pl.BlockSpec((tm, d), lambda i, j: (i, 0)),
            pl.BlockSpec((d, tf), lambda i, j: (0, j)),
            pl.BlockSpec((d, tf), lambda i, j: (0, j)),
            pl.BlockSpec((CONV_W, tf), lambda i, j: (0, j)),
            pl.BlockSpec((CONV_W, tf), lambda i, j: (0, nj + j)),
            pl.BlockSpec((tf, d), lambda i, j: (j, 0)),
            pl.BlockSpec((1, d), lambda i, j: (0, 0)),
        ],
        out_specs=[pl.BlockSpec((tm, d), lambda i, j: (i, 0)), st_spec, st_spec],
        out_shape=[jax.ShapeDtypeStruct((rows, d), F32), st_shape, st_shape],
        scratch_shapes=[
            pltpu.VMEM((tm, d), F32),
            pltpu.VMEM((nj, SUBLANES, tf), F32), pltpu.VMEM((nj, SUBLANES, tf), F32),
            pltpu.VMEM((tm + SUBLANES, tf), F32), pltpu.VMEM((tm + SUBLANES, tf), F32)],
        compiler_params=_cparams(("arbitrary", "arbitrary")),
        name=name,
    )(x, h, wg, wv, conv_w, conv_w, w_down, post_w)


def _s_proj_kernel(h_ref, w_ref, y_ref, wb_ref):
    wb = w_ref[...].astype(BF16)
    wb_ref[...] = wb
    y_ref[...] = jnp.dot(h_ref[...], wb, preferred_element_type=F32)


def _s_proj(h, w, col0, ncols, tn, name):
    nb, d = h.shape
    c0 = col0 // tn
    return pl.pallas_call(
        _s_proj_kernel,
        grid=(ncols // tn,),
        in_specs=[pl.BlockSpec((nb, d), lambda j: (0, 0)),
                  pl.BlockSpec((d, tn), lambda j: (0, c0 + j))],
        out_specs=[pl.BlockSpec((nb, tn), lambda j: (0, j)), pl.BlockSpec((d, tn), lambda j: (0, j))],
        out_shape=[jax.ShapeDtypeStruct((nb, ncols), F32), jax.ShapeDtypeStruct((d, ncols), BF16)],
        compiler_params=_cparams(("parallel",)),
        name=name,
    )(h, w)


def _s_conv_proj_kernel(h_ref, wu_ref, wb_ref, wc_ref, z_ref, bg_ref, wu_o, wb_o, wc_o):
    h = h_ref[...]
    wu = wu_ref[...].astype(BF16)
    wb = wb_ref[...].astype(BF16)
    wc = wc_ref[...].astype(BF16)
    wu_o[...] = wu
    wb_o[...] = wb
    wc_o[...] = wc
    u = jnp.dot(h, wu, preferred_element_type=F32)
    cg = jnp.dot(h, wc, preferred_element_type=F32)
    z_ref[...] = cg * u
    bg_ref[...] = jnp.dot(h, wb, preferred_element_type=F32)


def _s_conv_proj(h, w, d_conv, tn, name):
    nb, d = h.shape
    nt = d_conv // tn
    y_spec = pl.BlockSpec((nb, tn), lambda j: (0, j))
    w_out = pl.BlockSpec((d, tn), lambda j: (0, j))
    return pl.pallas_call(
        _s_conv_proj_kernel,
        grid=(nt,),
        in_specs=[pl.BlockSpec((nb, d), lambda j: (0, 0)),
                  pl.BlockSpec((d, tn), lambda j: (0, j)),
                  pl.BlockSpec((d, tn), lambda j: (0, nt + j)),
                  pl.BlockSpec((d, tn), lambda j: (0, 2 * nt + j))],
        out_specs=[y_spec, y_spec, w_out, w_out, w_out],
        out_shape=[jax.ShapeDtypeStruct((nb, d_conv), F32)] * 2
                  + [jax.ShapeDtypeStruct((d, d_conv), BF16)] * 3,
        compiler_params=_cparams(("parallel",)),
        name=name,
    )(h, w, w, w)


def _s_mix_kernel(z_ref, st0_ref, st1_ref, bg_ref, o_ref, ga_ref, gb_ref, cw_ref, wc_ref, wa_ref,
                  m_ref, wc_o, wa_o):
    cw = cw_ref[...]
    zc = st0_ref[...] * cw[0:1] + st1_ref[...] * cw[1:2] + z_ref[...] * cw[2:3]
    wc = wc_ref[...].astype(BF16)
    wa = wa_ref[...].astype(BF16)
    wc_o[...] = wc
    wa_o[...] = wa
    y_conv = jnp.dot((bg_ref[...] * zc).astype(BF16), wc, preferred_element_type=F32)
    y_attn = jnp.dot(o_ref[...], wa, preferred_element_type=F32)
    merged = jax.nn.sigmoid(ga_ref[...]) * y_conv + jax.nn.sigmoid(gb_ref[...]) * y_attn
    m_ref[...] = merged.astype(BF16)


def _s_mix(z, st0, st1, bg, o, gates, conv_w, w_conv_out, w_attn_out, tn, name):
    nb, dc = z.shape
    da, d = w_attn_out.shape
    nt = d // tn
    full = lambda a: pl.BlockSpec(a.shape, lambda j: (0, 0))
    return pl.pallas_call(
        _s_mix_kernel,
        grid=(nt,),
        in_specs=[full(z), full(st0), full(st1), full(bg), full(o),
                  pl.BlockSpec((nb, tn), lambda j: (0, j)),
                  pl.BlockSpec((nb, tn), lambda j: (0, nt + j)),
                  full(conv_w),
                  pl.BlockSpec((dc, tn), lambda j: (0, j)),
                  pl.BlockSpec((da, tn), lambda j: (0, j))],
        out_specs=[pl.BlockSpec((nb, tn), lambda j: (0, j)),
                   pl.BlockSpec((dc, tn), lambda j: (0, j)),
                   pl.BlockSpec((da, tn), lambda j: (0, j))],
        out_shape=[jax.ShapeDtypeStruct((nb, d), BF16),
                   jax.ShapeDtypeStruct(w_conv_out.shape, BF16),
                   jax.ShapeDtypeStruct(w_attn_out.shape, BF16)],
        compiler_params=_cparams(("parallel",)),
        name=name,
    )(z, st0, st1, bg, o, gates, gates, conv_w, w_conv_out, w_attn_out)


def _s_out_kernel(m_ref, wo_ref, x_ref, pw_ref, fw_ref, out_ref, h_ref, wo_o, acc_ref):
    k = pl.program_id(0)

    @pl.when(k == 0)
    def _():
        acc_ref[...] = jnp.zeros_like(acc_ref)

    wo = wo_ref[...].astype(BF16)
    wo_o[...] = wo
    acc_ref[...] += jnp.dot(m_ref[...], wo, preferred_element_type=F32)

    @pl.when(k == pl.num_programs(0) - 1)
    def _():
        x_new = x_ref[...] + _rms(acc_ref[...], pw_ref[...])
        out_ref[...] = x_new
        h_ref[...] = _rms(x_new, fw_ref[...]).astype(BF16)


def _s_out(merged, w_o, x, post_w, ffn_w, tk, name):
    nb, d = x.shape
    full = lambda a: pl.BlockSpec(a.shape, lambda k: (0, 0))
    return pl.pallas_call(
        _s_out_kernel,
        grid=(d // tk,),
        in_specs=[pl.BlockSpec((nb, tk), lambda k: (0, k)),
                  pl.BlockSpec((tk, d), lambda k: (k, 0)),
                  full(x), full(post_w), full(ffn_w)],
        out_specs=[full(x), full(x), pl.BlockSpec((tk, d), lambda k: (k, 0))],
        out_shape=[jax.ShapeDtypeStruct((nb, d), F32), jax.ShapeDtypeStruct((nb, d), BF16),
                   jax.ShapeDtypeStruct(w_o.shape, BF16)],
        scratch_shapes=[pltpu.VMEM((nb, d), F32)],
        compiler_params=_cparams(("arbitrary",)),
        name=name,
    )(merged, w_o, x, post_w, ffn_w)


def _s_ffn_kernel(x_ref, h_ref, wg_ref, wv_ref, cwg_ref, cwv_ref, wd_ref, pw_ref,
                  g0_ref, g1_ref, v0_ref, v1_ref,
                  out_ref, ug_ref, uv_ref, wg_o, wv_o, wd_o, acc_ref):
    j = pl.program_id(0)

    @pl.when(j == 0)
    def _():
        acc_ref[...] = jnp.zeros_like(acc_ref)

    h = h_ref[...]
    wg = wg_ref[...].astype(BF16)
    wv = wv_ref[...].astype(BF16)
    wd = wd_ref[...].astype(BF16)
    wg_o[...] = wg
    wv_o[...] = wv
    wd_o[...] = wd
    up_g = jnp.dot(h, wg, preferred_element_type=F32)
    up_v = jnp.dot(h, wv, preferred_element_type=F32)
    ug_ref[...] = up_g
    uv_ref[...] = up_v
    cwg = cwg_ref[...]
    cwv = cwv_ref[...]
    gate = g0_ref[...] * cwg[0:1] + g1_ref[...] * cwg[1:2] + up_g * cwg[2:3]
    val = v0_ref[...] * cwv[0:1] + v1_ref[...] * cwv[1:2] + up_v * cwv[2:3]
    act = (gate * jax.nn.sigmoid(gate)) * val
    acc_ref[...] += jnp.dot(act.astype(BF16), wd, preferred_element_type=F32)

    @pl.when(j == pl.num_programs(0) - 1)
    def _():
        out_ref[...] = x_ref[...] + _rms(acc_ref[...], pw_ref[...])


def _s_ffn(x, h, w_up, conv_w, w_down, post_w, prev, tf, name):
    nb, d = x.shape
    d_ff = w_down.shape[0]
    nj = d_ff // tf
    full = lambda a: pl.BlockSpec(a.shape, lambda j: (0, 0))
    col = pl.BlockSpec((nb, tf), lambda j: (0, j))
    up_w = pl.BlockSpec((d, tf), lambda j: (0, j))
    down_w = pl.BlockSpec((tf, d), lambda j: (j, 0))
    return pl.pallas_call(
        _s_ffn_kernel,
        grid=(nj,),
        in_specs=[full(x), full(h), up_w, pl.BlockSpec((d, tf), lambda j: (0, nj + j)),
                  pl.BlockSpec((CONV_W, tf), lambda j: (0, j)),
                  pl.BlockSpec((CONV_W, tf), lambda j: (0, nj + j)),
                  down_w, full(post_w), col, col, col, col],
        out_specs=[full(x), col, col, up_w, up_w, down_w],
        out_shape=[jax.ShapeDtypeStruct((nb, d), F32),
                   jax.ShapeDtypeStruct((nb, d_ff), F32), jax.ShapeDtypeStruct((nb, d_ff), F32),
                   jax.ShapeDtypeStruct((d, d_ff), BF16), jax.ShapeDtypeStruct((d, d_ff), BF16),
                   jax.ShapeDtypeStruct((d_ff, d), BF16)],
        scratch_shapes=[pltpu.VMEM((nb, d), F32)],
        compiler_params=_cparams(("arbitrary",)),
        name=name,
    )(x, h, w_up, w_up, conv_w, conv_w, w_down, post_w, *prev)


def kernel(x_prompt, x_sample, cache_k, cache_v, page_table, state_conv, state_ffn_conv,
           pre_mix_w, w_in, conv_w, w_conv_out, lambda_q1, lambda_k1, lambda_q2, lambda_k2,
           subln_w, w_attn_out, w_o, post_mix_w, pre_ffn_w, w_up, ffn_conv_w, w_down, post_ffn_w):
    batch, seq, d = x_prompt.shape
    nb = x_sample.shape[0]
    depth = w_in.shape[0]
    n_heads = cache_k.shape[3]
    kd = cache_k.shape[4]
    d_conv = conv_w.shape[2]
    d_ff = w_down.shape[1]
    rows = batch * seq
    qkv = n_heads * kd
    c_q, c_k, c_v, c_g = 3 * d_conv, 3 * d_conv + qkv, 3 * d_conv + 2 * qkv, 3 * d_conv + 3 * qkv
    slopes = jnp.exp2(-8.0 * jnp.arange(1, n_heads + 1, dtype=F32) / n_heads)

    xp = x_prompt.reshape(rows, d)
    xs = x_sample.reshape(nb, d)
    outs = {n: [] for n in ("kp", "vp", "cp", "fp", "ks", "vs", "cs", "fs")}
    tm, tn = 1024, 1024
    tm_merge, tm_ffn, tf = 256, 512, 512
    ts = 512

    for l in range(depth):
        lam_init = 0.8 - 0.6 * math.exp(-0.3 * l)
        row2 = lambda a: a[l].reshape(1, -1)
        lams = [row2(lambda_q1), row2(lambda_k1), row2(lambda_q2), row2(lambda_k2)]
        pre_w, post_w = row2(pre_mix_w), row2(post_mix_w)
        pre_f, post_f, sub_w = row2(pre_ffn_w), row2(post_ffn_w), row2(subln_w)

        hs = _norm(xs, pre_w, nb)
        zs, bgs, wu_b, wb_b, wcg_b = _s_conv_proj(hs, w_in[l], d_conv, ts, "s_conv_proj")
        qkv_s, wqkv_b = _s_proj(hs, w_in[l], c_q, 3 * qkv, ts, "s_qkv_proj")
        gates_s, wg_b = _s_proj(hs, w_in[l], c_g, 2 * d, ts, "s_gate_proj")
        q_s = qkv_s[:, :qkv].reshape(nb, n_heads, kd)
        k_s = qkv_s[:, qkv:2 * qkv].reshape(nb, n_heads, kd)
        v_s = qkv_s[:, 2 * qkv:].reshape(nb, n_heads, kd)
        o_s = _decode_attention(q_s, k_s, v_s, cache_k, cache_v, page_table, l, slopes, lams,
                                sub_w, lam_init)
        sc = state_conv[l]
        merged_s, wc_b, wa_b = _s_mix(zs, sc[:, 0], sc[:, 1], bgs, o_s.reshape(nb, qkv).astype(BF16),
                                      gates_s, conv_w[l], w_conv_out[l], w_attn_out[l], ts, "s_mix")
        xs, hfs, wo_b = _s_out(merged_s, w_o[l], xs, post_w, pre_f, ts, "s_out")
        sf = state_ffn_conv[l]
        prev = (sf[:, 0, :d_ff], sf[:, 1, :d_ff], sf[:, 0, d_ff:], sf[:, 1, d_ff:])
        xs, ug, uv, wug_b, wuv_b, wd_b = _s_ffn(xs, hfs, w_up[l], ffn_conv_w[l], w_down[l], post_f,
                                                prev, tf, "s_ffn")
        outs["ks"].append(k_s.reshape(nb, 1, n_heads, kd))
        outs["vs"].append(v_s.reshape(nb, 1, n_heads, kd))
        outs["cs"].append(jnp.stack([sc[:, 1], zs], axis=1))
        outs["fs"].append(jnp.stack([sf[:, 1], jnp.concatenate([ug, uv], axis=-1)], axis=1))

        hp = _norm(xp, pre_w, 512)
        z, bg = _conv_proj(hp, wu_b, wb_b, wcg_b, tm, 512, "p_conv_proj")
        (q_b,) = _proj(hp, wqkv_b, 0, qkv, "bf16", tm, tn, "p_q_proj")
        k_f, k_b = _proj(hp, wqkv_b, qkv, qkv, "both", tm, tn, "p_k_proj")
        v_f, v_b = _proj(hp, wqkv_b, 2 * qkv, qkv, "both", tm, tn, "p_v_proj")
        (gates,) = _proj(hp, wg_b, 0, 2 * d, "f32", tm, tn, "p_gate_proj")
        o = _flash_attention(q_b, k_b, v_b, slopes, lams, sub_w, batch, seq, n_heads, lam_init)
        xp, hf = _merge(z, bg, o, gates, xp, conv_w[l], wc_b, wa_b, wo_b, post_w, pre_f,
                        tm_merge, seq, "p_merge")
        xp, fg, fv = _ffn(xp, hf, wug_b, wuv_b, ffn_conv_w[l], wd_b, post_f, tm_ffn, seq, tf, "p_ffn")
        last = seq // tm_ffn - 1
        outs["kp"].append(k_f.reshape(batch, seq, n_heads, kd))
        outs["vp"].append(v_f.reshape(batch, seq, n_heads, kd))
        outs["cp"].append(z.reshape(batch, seq, d_conv)[:, seq - (CONV_W - 1):])
        outs["fp"].append(jnp.concatenate([fg, fv], axis=-1)[last::seq // tm_ffn])

    st = lambda n: jnp.stack(outs[n])
    return (xp.reshape(batch, seq, d), xs.reshape(nb, 1, d),
            st("kp"), st("vp"), st("cp"), st("fp"),
            st("ks"), st("vs"), st("cs"), st("fs"))
```

```python
import functools
import math

import jax
import jax.numpy as jnp
import numpy as np
from jax import lax
from jax.experimental import pallas as pl
from jax.experimental.pallas import tpu as pltpu

F32 = jnp.float32
BF16 = jnp.bfloat16

HEAD_DIM = 128
CONV_W = 3
RMS_EPS = 1e-6
NEG_BIG = -1e30
LOG2E = 1.4426950408889634
LANES = 128
SUBLANES = 8
VMEM_LIMIT = 56 * 1024 * 1024
PARTIAL_SUMS = 1
FFN_SUB = 256


def _bf16_pieces(x, n):
    pieces = []
    for _ in range(n):
        p = float(np.asarray(x, np.float32).astype(BF16))
        pieces.append(p)
        x -= p
    return tuple(pieces)


SQRT_HEAD_DIM_PIECES = _bf16_pieces(HEAD_DIM ** 0.5, 3)


def _cparams(sem):
    return pltpu.CompilerParams(dimension_semantics=sem, vmem_limit_bytes=VMEM_LIMIT)


def _rms(x, w):
    ms = jnp.mean(x * x, axis=-1, keepdims=True)
    return x * lax.rsqrt(ms + RMS_EPS) * w


def _norm_kernel(x_ref, nw_ref, h_ref):
    h_ref[...] = _rms(x_ref[...], nw_ref[...]).astype(BF16)


def _norm(x, nw, tm):
    rows, d = x.shape
    return pl.pallas_call(
        _norm_kernel,
        grid=(rows // tm,),
        in_specs=[pl.BlockSpec((tm, d), lambda i: (i, 0)), pl.BlockSpec((1, d), lambda i: (0, 0))],
        out_specs=pl.BlockSpec((tm, d), lambda i: (i, 0)),
        out_shape=jax.ShapeDtypeStruct((rows, d), BF16),
        compiler_params=_cparams(("parallel",)),
        name="rmsnorm",
    )(x, nw)


def _proj_kernel(h_ref, w_ref, o_ref):
    y = jnp.dot(h_ref[...], w_ref[...], preferred_element_type=F32)
    o_ref[...] = y.astype(o_ref.dtype)


def _proj(h, w, col0, ncols, out_dtype, tm, tn, name):
    rows, d = h.shape
    c0 = col0 // tn
    return pl.pallas_call(
        _proj_kernel,
        grid=(rows // tm, ncols // tn),
        in_specs=[
            pl.BlockSpec((tm, d), lambda i, j: (i, 0)),
            pl.BlockSpec((d, tn), lambda i, j: (0, c0 + j)),
        ],
        out_specs=pl.BlockSpec((tm, tn), lambda i, j: (i, j)),
        out_shape=jax.ShapeDtypeStruct((rows, ncols), out_dtype),
        compiler_params=_cparams(("parallel", "parallel")),
        name=name,
    )(h, w)


def _kv_proj_kernel(h_ref, w_ref, o_ref, ob_ref, *, tm, n_heads):
    y = jnp.dot(h_ref[...], w_ref[...], preferred_element_type=F32)
    ob_ref[...] = y.astype(BF16)
    o_ref[...] = pltpu.einshape("m(hd)->mhd", y, h=n_heads)


def _kv_proj(h, w, col_block, n_heads, kd, tm, name):
    rows, d = h.shape
    width = n_heads * kd
    return pl.pallas_call(
        functools.partial(_kv_proj_kernel, tm=tm, n_heads=n_heads),
        grid=(rows // tm,),
        in_specs=[pl.BlockSpec((tm, d), lambda i: (i, 0)),
                  pl.BlockSpec((d, width), lambda i: (0, col_block))],
        out_specs=[pl.BlockSpec((tm, n_heads, kd), lambda i: (i, 0, 0)),
                   pl.BlockSpec((tm, width), lambda i: (i, 0))],
        out_shape=[jax.ShapeDtypeStruct((rows, n_heads, kd), F32),
                   jax.ShapeDtypeStruct((rows, width), BF16)],
        compiler_params=_cparams(("parallel",)),
        name=name,
    )(h, w)


def _conv_proj_kernel(h_ref, wu_ref, wb_ref, wc_ref, z_ref, bg_ref):
    h = h_ref[...]
    u = jnp.dot(h, wu_ref[...], preferred_element_type=F32)
    cg = jnp.dot(h, wc_ref[...], preferred_element_type=F32)
    z_ref[...] = cg * u
    bg_ref[...] = jnp.dot(h, wb_ref[...], preferred_element_type=F32)


def _conv_proj(h, wu, wb, wc, tm, tn, name):
    rows, d = h.shape
    d_conv = wu.shape[1]
    w_spec = pl.BlockSpec((d, tn), lambda i, j: (0, j))
    return pl.pallas_call(
        _conv_proj_kernel,
        grid=(rows // tm, d_conv // tn),
        in_specs=[pl.BlockSpec((tm, d), lambda i, j: (i, 0)), w_spec, w_spec, w_spec],
        out_specs=[pl.BlockSpec((tm, tn), lambda i, j: (i, j))] * 2,
        out_shape=[jax.ShapeDtypeStruct((rows, d_conv), F32)] * 2,
        compiler_params=_cparams(("parallel", "parallel")),
        name=name,
    )(h, wu, wb, wc)


def _lambda_value(lq1_ref, lk1_ref, lq2_ref, lk2_ref, lam_init):
    a = jnp.sum(lq1_ref[...] * lk1_ref[...], axis=-1, keepdims=True)
    b = jnp.sum(lq2_ref[...] * lk2_ref[...], axis=-1, keepdims=True)
    return jnp.exp(a) - jnp.exp(b) + lam_init


def _head_out(o, sub_w, lam_init):
    return _rms(o, sub_w) * (1.0 - lam_init)


def _flash_kernel(slope_ref, q_ref, k_ref, v_ref, lq1_ref, lk1_ref, lq2_ref, lk2_ref,
                  subt_ref, o_ref, acc1_ref, acc2_ref, vt_ref, sa1_ref, sa2_ref, sb1_ref, sb2_ref,
                  *, tq, lam_init):
    h = pl.program_id(1)
    qi = pl.program_id(2)
    slope = slope_ref[h]
    scale2 = HEAD_DIM ** -0.5 * LOG2E

    lane = lax.broadcasted_iota(jnp.int32, (tq, HEAD_DIM), 1)
    pos = lax.broadcasted_iota(jnp.int32, (tq, HEAD_DIM), 0)
    n_pc = len(SQRT_HEAD_DIM_PIECES)
    low = pos & 1
    k_feat = jnp.where(lane < n_pc, pos - low, jnp.where(lane < 2 * n_pc, low, 0))
    aug_k = (k_feat.astype(F32) * slope).astype(BF16)
    q_feat = jnp.zeros((tq, HEAD_DIM), F32)
    for n, piece in enumerate(SQRT_HEAD_DIM_PIECES):
        q_feat = jnp.where((lane == n) | (lane == n + n_pc), piece, q_feat)
    aug_q = q_feat.astype(BF16)
    qa = [jnp.concatenate([q_ref[:, c0:c0 + HEAD_DIM], aug_q], axis=1) for c0 in (0, HEAD_DIM)]

    tk = tq // 2
    slope2 = slope * LOG2E

    @pl.when(qi == 0)
    def _():
        for blk in range(vt_ref.shape[0]):
            vt_ref[blk] = v_ref[blk * tk:(blk + 1) * tk, :].astype(F32).T.astype(BF16)

    acc_refs = (acc1_ref, acc2_ref)
    acc1_ref[...] = jnp.zeros_like(acc1_ref)
    acc2_ref[...] = jnp.zeros_like(acc2_ref)

    def qk_store(kb, dst_refs):
        off = pl.multiple_of(kb * tk, tk)
        k = k_ref[pl.ds(off, tk), :]
        for n in range(2):
            ka = jnp.concatenate([k[:, n * HEAD_DIM:(n + 1) * HEAD_DIM], aug_k[:tk]], axis=1)
            s = lax.dot_general(ka, qa[n], (((1,), (1,)), ((), ())), preferred_element_type=F32)
            dst_refs[n][...] = s * scale2

    def consume(kb, src_refs, carry, mask_shift):
        vt = vt_ref[kb]
        c = slope2 * (kb * tk - qi * tq).astype(F32)
        new_carry, alphas, ps = [], [], []
        for n in range(2):
            m, l = carry[2 * n], carry[2 * n + 1]
            s = src_refs[n][...]
            if mask_shift is not None:
                key = lax.broadcasted_iota(jnp.int32, (tk, tq), 0)
                qry = lax.broadcasted_iota(jnp.int32, (tk, tq), 1)
                s = jnp.where(key + mask_shift <= qry, s, NEG_BIG)
            m_new = jnp.maximum(m, jnp.max(s, axis=0, keepdims=True) + c)
            alpha = jnp.exp2(m - m_new)
            p = jnp.exp2(s - (m_new - c))
            new_carry += [m_new, alpha * l + jnp.sum(p, axis=0, keepdims=True)]
            alphas.append(alpha)
            ps.append(p.astype(BF16))
        for n in range(2):
            acc_refs[n][...] = alphas[n] * acc_refs[n][...] + jnp.dot(
                vt, ps[n], preferred_element_type=F32)
        return tuple(new_carry)

    buf_a = (sa1_ref, sa2_ref)
    buf_b = (sb1_ref, sb2_ref)
    qk_store(0, buf_a)

    def pair(t, carry):
        kb = 2 * t
        qk_store(kb + 1, buf_b)
        carry = consume(kb, buf_a, carry, None)
        qk_store(kb + 2, buf_a)
        return consume(kb + 1, buf_b, carry, None)

    m0 = jnp.full((1, tq), NEG_BIG, F32)
    l0 = jnp.zeros((1, tq), F32)
    carry = lax.fori_loop(0, qi, pair, (m0, l0, m0, l0))
    qk_store(2 * qi + 1, buf_b)
    carry = consume(2 * qi, buf_a, carry, 0)
    m1, l1, m2, l2 = consume(2 * qi + 1, buf_b, carry, tk)

    lam = _lambda_value(lq1_ref, lk1_ref, lq2_ref, lk2_ref, lam_init)
    o_t = acc1_ref[...] / l1 - lam * (acc2_ref[...] / l2)
    ms = jnp.mean(o_t * o_t, axis=0, keepdims=True)
    y_t = o_t * lax.rsqrt(ms + RMS_EPS) * subt_ref[...] * (1.0 - lam_init)
    o_ref[...] = y_t.T.astype(BF16)


def _flash_attention(q, k, v, slopes, lams, sub_w, batch, seq, n_heads, lam_init, tq=512):
    kd = 2 * HEAD_DIM
    nq = seq // tq
    lam_spec = pl.BlockSpec((1, HEAD_DIM), lambda b, h, i: (0, 0))
    return pl.pallas_call(
        functools.partial(_flash_kernel, tq=tq, lam_init=lam_init),
        grid=(batch, n_heads, nq),
        in_specs=[
            pl.BlockSpec(memory_space=pltpu.SMEM),
            pl.BlockSpec((tq, kd), lambda b, h, i: (b * nq + i, h)),
            pl.BlockSpec((seq, kd), lambda b, h, i: (b, h)),
            pl.BlockSpec((seq, kd), lambda b, h, i: (b, h)),
            lam_spec, lam_spec, lam_spec, lam_spec,
            pl.BlockSpec((kd, 1), lambda b, h, i: (0, 0)),
        ],
        out_specs=pl.BlockSpec((tq, kd), lambda b, h, i: (b * nq + i, h)),
        out_shape=jax.ShapeDtypeStruct(q.shape, BF16),
        scratch_shapes=[pltpu.VMEM((kd, tq), F32), pltpu.VMEM((kd, tq), F32),
                        pltpu.VMEM((2 * nq, kd, tq // 2), BF16)]
                       + [pltpu.VMEM((tq // 2, tq), F32)] * 4,
        compiler_params=_cparams(("parallel", "parallel", "arbitrary")),
        name="prompt_attention",
    )(slopes, q, k, v, *lams, sub_w.reshape(kd, 1))


def _decode_kernel(pt_ref, q_ref, kn_ref, vn_ref, lq1_ref, lk1_ref, lq2_ref, lk2_ref,
                   sub_ref, slope_ref, tbl_ref, *rest, pages_per_step, page, n_pages,
                   lam_init, tokens_per_iter):
    del pt_ref
    k_refs = rest[:pages_per_step]
    v_refs = rest[pages_per_step:2 * pages_per_step]
    o_ref, m_ref, l_ref, acc_ref, s_even_ref, s_odd_ref = rest[2 * pages_per_step:]
    s_refs = (s_even_ref, s_odd_ref)
    g = pl.program_id(1)
    past = n_pages * page
    n_iter = page // tokens_per_iter

    @pl.when(g == 0)
    def _():
        m_ref[...] = jnp.full_like(m_ref, NEG_BIG)
        l_ref[...] = jnp.zeros_like(l_ref)
        acc_ref[...] = jnp.zeros_like(acc_ref)

    qv = q_ref[0] * (HEAD_DIM ** -0.5 * LOG2E)
    q1 = qv[:, :HEAD_DIM]
    q2 = qv[:, HEAD_DIM:]
    slope2 = slope_ref[...]

    def rowsum(x):
        return jnp.broadcast_to(jnp.sum(x, axis=-1, keepdims=True), x.shape)

    def twice(p):
        return jnp.concatenate([p, p], axis=-1)

    def run(score_page, acc_page, shifts):
        if score_page is not None:
            k_ref, sa_ref = k_refs[score_page], s_refs[score_page % 2]
        if acc_page is not None:
            v_ref, sb_ref = v_refs[acc_page], s_refs[acc_page % 2]

        def body(it, carry):
            mx1, mx2, parts = carry
            parts = [list(p) for p in parts]
            base = it * tokens_per_iter
            for tt in range(tokens_per_iter):
                t = base + tt
                if score_page is not None:
                    kt = k_ref[t]
                    tb = tbl_ref[t]
                    u1 = rowsum(kt[:, :HEAD_DIM] * q1) + tb
                    u2 = rowsum(kt[:, HEAD_DIM:] * q2) + tb
                    sa_ref[t, 0] = u1
                    sa_ref[t, 1] = u2
                    mx1 = jnp.maximum(mx1, u1)
                    mx2 = jnp.maximum(mx2, u2)
                if acc_page is not None:
                    p1 = jnp.exp2(sb_ref[t, 0] - shifts[0])
                    p2 = jnp.exp2(sb_ref[t, 1] - shifts[1])
                    vt = v_ref[t]
                    part = parts[tt % PARTIAL_SUMS]
                    part[0] = part[0] + p1
                    part[1] = part[1] + p2
                    part[2] = part[2] + twice(p1) * vt
                    part[3] = part[3] + twice(p2) * vt
            return mx1, mx2, tuple(tuple(p) for p in parts)

        neg = jnp.full((SUBLANES, LANES), NEG_BIG, F32)
        zero = (jnp.zeros_like(l_ref[0]), jnp.zeros_like(l_ref[0]),
                jnp.zeros_like(acc_ref[0]), jnp.zeros_like(acc_ref[0]))
        first = (l_ref[0], l_ref[1], acc_ref[0], acc_ref[1])
        if acc_page is None:
            parts0 = ()
        else:
            parts0 = (first,) + (zero,) * (PARTIAL_SUMS - 1)
        mx1, mx2, parts = lax.fori_loop(0, n_iter, body, (neg, neg, parts0))
        if acc_page is not None:
            l_ref[0] = sum(p[0] for p in parts[1:]) + parts[0][0]
            l_ref[1] = sum(p[1] for p in parts[1:]) + parts[0][1]
            acc_ref[0] = sum(p[2] for p in parts[1:]) + parts[0][2]
            acc_ref[1] = sum(p[3] for p in parts[1:]) + parts[0][3]
        return mx1, mx2

    def new_max(c, m_cand):
        m_new = jnp.maximum(m_ref[c], m_cand)
        alpha = jnp.exp2(m_ref[c] - m_new)
        m_ref[c] = m_new
        l_ref[c] = alpha * l_ref[c]
        acc_ref[c] = twice(alpha) * acc_ref[c]
        return m_new

    def page_shifts(r, mx1, mx2):
        first_pos = (g * pages_per_step + r) * page
        c = slope2 * (first_pos - past).astype(F32)
        return new_max(0, mx1 + c) - c, new_max(1, mx2 + c) - c

    mx = run(0, None, None)
    for r in range(pages_per_step):
        shifts = page_shifts(r, *mx)
        if r + 1 < pages_per_step:
            mx = run(r + 1, r, shifts)
        else:
            run(None, r, shifts)

    @pl.when(g == pl.num_programs(1) - 1)
    def _():
        kn = kn_ref[0]
        vn = vn_ref[0]
        s1 = rowsum(kn[:, :HEAD_DIM] * q1)
        s2 = rowsum(kn[:, HEAD_DIM:] * q2)
        p1 = jnp.exp2(s1 - new_max(0, s1))
        p2 = jnp.exp2(s2 - new_max(1, s2))
        o1 = (acc_ref[0] + twice(p1) * vn) / twice(l_ref[0] + p1)
        o2 = (acc_ref[1] + twice(p2) * vn) / twice(l_ref[1] + p2)
        lam = _lambda_value(lq1_ref, lk1_ref, lq2_ref, lk2_ref, lam_init)
        o_ref[0] = _head_out(o1 - lam * o2, sub_ref[...], lam_init)


def _decode_attention(q, k_new, v_new, cache_k, cache_v, page_table, layer, slopes, lams,
                      sub_w, lam_init, pages_per_step=8, tokens_per_iter=128):
    nb, n_heads, kd = q.shape
    n_pages = page_table.shape[1]
    page = cache_k.shape[2]
    steps = n_pages // pages_per_step
    slope2 = jnp.broadcast_to((slopes * LOG2E)[:, None], (n_heads, LANES))
    tok_bias = jnp.arange(page, dtype=F32)[:, None, None] * slope2[None]

    tok_spec = pl.BlockSpec((1, n_heads, kd), lambda b, g, pt: (b, 0, 0))
    lam_spec = pl.BlockSpec((1, HEAD_DIM), lambda b, g, pt: (0, 0))

    def page_spec(r):
        return pl.BlockSpec(
            (None, None, page, n_heads, kd),
            lambda b, g, pt: (layer, pt[b, g * pages_per_step + r], 0, 0, 0))

    page_specs = [page_spec(r) for r in range(pages_per_step)]
    grid_spec = pltpu.PrefetchScalarGridSpec(
        num_scalar_prefetch=1,
        grid=(nb, steps),
        in_specs=[tok_spec, tok_spec, tok_spec, lam_spec, lam_spec, lam_spec, lam_spec,
                  pl.BlockSpec((1, kd), lambda b, g, pt: (0, 0)),
                  pl.BlockSpec((n_heads, LANES), lambda b, g, pt: (0, 0)),
                  pl.BlockSpec((page, n_heads, LANES), lambda b, g, pt: (0, 0, 0))]
                 + page_specs + page_specs,
        out_specs=tok_spec,
        scratch_shapes=[
            pltpu.VMEM((2, n_heads, LANES), F32),
            pltpu.VMEM((2, n_heads, LANES), F32),
            pltpu.VMEM((2, n_heads, kd), F32),
            pltpu.VMEM((page, 2, n_heads, LANES), F32),
            pltpu.VMEM((page, 2, n_heads, LANES), F32),
        ],
    )
    return pl.pallas_call(
        functools.partial(_decode_kernel, pages_per_step=pages_per_step, page=page,
                          n_pages=n_pages, lam_init=lam_init, tokens_per_iter=tokens_per_iter),
        grid_spec=grid_spec,
        out_shape=jax.ShapeDtypeStruct((nb, n_heads, kd), F32),
        compiler_params=_cparams(("parallel", "arbitrary")),
        name="sample_attention",
    )(page_table, q, k_new, v_new, *lams, sub_w, slope2, tok_bias,
      *([cache_k] * pages_per_step), *([cache_v] * pages_per_step))


def _merge_kernel(z_ref, halo_ref, bg_ref, o_ref, ga_ref, gb_ref, x_ref, cw_ref, wc_ref, wa_ref,
                  wo_ref, pw_ref, fw_ref, out_ref, h_ref, zbuf_ref, *, tm, tiles_per_seq):
    i = pl.program_id(0)
    z = z_ref[...]
    starts_seq = i % tiles_per_seq == 0
    zbuf_ref[0:SUBLANES, :] = jnp.where(starts_seq, 0.0, halo_ref[...])
    zbuf_ref[SUBLANES:, :] = z
    z1 = zbuf_ref[SUBLANES - 1:SUBLANES - 1 + tm, :]
    z2 = zbuf_ref[SUBLANES - 2:SUBLANES - 2 + tm, :]
    cw = cw_ref[...]
    zc = z2 * cw[0:1] + z1 * cw[1:2] + z * cw[2:3]
    y_conv = jnp.dot((bg_ref[...] * zc).astype(BF16), wc_ref[...], preferred_element_type=F32)
    y_attn = jnp.dot(o_ref[...], wa_ref[...], preferred_element_type=F32)
    merged = jax.nn.sigmoid(ga_ref[...]) * y_conv + jax.nn.sigmoid(gb_ref[...]) * y_attn
    mo = jnp.dot(merged.astype(BF16), wo_ref[...], preferred_element_type=F32)
    x_new = x_ref[...] + _rms(mo, pw_ref[...])
    out_ref[...] = x_new
    h_ref[...] = _rms(x_new, fw_ref[...]).astype(BF16)


def _merge(z, bg, o, gates, x, conv_w, wc, wa, wo, post_w, ffn_w, tm, seq, name):
    rows, d = x.shape
    dc = z.shape[1]
    row = lambda w: pl.BlockSpec((tm, w), lambda i: (i, 0))
    const = lambda a: pl.BlockSpec(a.shape, lambda i: (0, 0), pipeline_mode=pl.Buffered(1))
    hb = tm // SUBLANES
    halo = pl.BlockSpec((SUBLANES, dc), lambda i: (jnp.maximum(i * hb - 1, 0), 0))
    return pl.pallas_call(
        functools.partial(_merge_kernel, tm=tm, tiles_per_seq=seq // tm),
        grid=(rows // tm,),
        in_specs=[row(dc), halo, row(dc), row(d),
                  pl.BlockSpec((tm, d), lambda i: (i, 0)),
                  pl.BlockSpec((tm, d), lambda i: (i, 1)),
                  row(d), const(conv_w), const(wc), const(wa), const(wo), const(post_w),
                  const(ffn_w)],
        out_specs=[row(d), row(d)],
        out_shape=[jax.ShapeDtypeStruct((rows, d), F32), jax.ShapeDtypeStruct((rows, d), BF16)],
        scratch_shapes=[pltpu.VMEM((tm + SUBLANES, dc), F32)],
        compiler_params=_cparams(("arbitrary",)),
        name=name,
    )(z, z, bg, o, gates, gates, x, conv_w, wc, wa, wo, post_w, ffn_w)


def _ffn_kernel(x_ref, h_ref, wg_ref, wv_ref, cwg_ref, cwv_ref, wd_ref, pw_ref,
                out_ref, sg_ref, sv_ref, acc_ref, cg_ref, cv_ref, ug_ref, uv_ref,
                *, tm, tiles_per_seq):
    i = pl.program_id(0)
    j = pl.program_id(1)

    @pl.when(j == 0)
    def _():
        acc_ref[...] = jnp.zeros_like(acc_ref)

        @pl.when(i % tiles_per_seq == 0)
        def _():
            cg_ref[...] = jnp.zeros_like(cg_ref)
            cv_ref[...] = jnp.zeros_like(cv_ref)

    h = h_ref[...]
    tf = wd_ref.shape[0]

    def up_proj(cs, w_ref, carry_ref, ubuf_ref, state_ref):
        up = jnp.dot(h, w_ref[:, cs], preferred_element_type=F32)
        ubuf_ref[0:SUBLANES, cs] = carry_ref[j, :, cs]
        ubuf_ref[SUBLANES:, cs] = up
        carry_ref[j, :, cs] = up[tm - SUBLANES:, :]
        state_ref[:, cs] = up[tm - (CONV_W - 1):, :]

    def conv(cs, cw_ref, ubuf_ref):
        cw = cw_ref[:, cs]
        u0 = ubuf_ref[SUBLANES:SUBLANES + tm, cs]
        u1 = ubuf_ref[SUBLANES - 1:SUBLANES - 1 + tm, cs]
        u2 = ubuf_ref[SUBLANES - 2:SUBLANES - 2 + tm, cs]
        return u2 * cw[0:1] + u1 * cw[1:2] + u0 * cw[2:3]

    chunks = [slice(c0, c0 + FFN_SUB) for c0 in range(0, tf, FFN_SUB)]
    for cs in chunks:
        up_proj(cs, wg_ref, cg_ref, ug_ref, sg_ref)
        up_proj(cs, wv_ref, cv_ref, uv_ref, sv_ref)
    down = None
    for cs in chunks:
        gate = conv(cs, cwg_ref, ug_ref)
        val = conv(cs, cwv_ref, uv_ref)
        act = (gate * jax.nn.sigmoid(gate)) * val
        part = jnp.dot(act.astype(BF16), wd_ref[cs, :], preferred_element_type=F32)
        down = part if down is None else down + part
    acc_ref[...] += down

    @pl.when(j == pl.num_programs(1) - 1)
    def _():
        out_ref[...] = x_ref[...] + _rms(acc_ref[...], pw_ref[...])


def _ffn(x, h, wg, wv, conv_w, w_down, post_w, tm, seq, tf, name):
    rows, d = x.shape
    d_ff = w_down.shape[0]
    nj = d_ff // tf
    st_spec = pl.BlockSpec((None, CONV_W - 1, tf), lambda i, j: (i, 0, j))
    st_shape = jax.ShapeDtypeStruct((rows // tm, CONV_W - 1, d_ff), F32)
    return pl.pallas_call(
        functools.partial(_ffn_kernel, tm=tm, tiles_per_seq=seq // tm),
        grid=(rows // tm, nj),
        in_specs=[
            pl.BlockSpec((tm, d), lambda i, j: (i, 0)),
            pl.BlockSpec((tm, d), lambda i, j: (i, 0)),
            pl.BlockSpec((d, tf), lambda i, j: (0, j)),
            pl.BlockSpec((d, tf), lambda i, j: (0, j)),
            pl.BlockSpec((CONV_W, tf), lambda i, j: (0, j)),
            pl.BlockSpec((CONV_W, tf), lambda i, j: (0, nj + j)),
            pl.BlockSpec((tf, d), lambda i, j: (j, 0)),
            pl.BlockSpec((1, d), lambda i, j: (0, 0)),
        ],
        out_specs=[pl.BlockSpec((tm, d), lambda i, j: (i, 0)), st_spec, st_spec],
        out_shape=[jax.ShapeDtypeStruct((rows, d), F32), st_shape, st_shape],
        scratch_shapes=[
            pltpu.VMEM((tm, d), F32),
            pltpu.VMEM((nj, SUBLANES, tf), F32), pltpu.VMEM((nj, SUBLANES, tf), F32),
            pltpu.VMEM((tm + SUBLANES, tf), F32), pltpu.VMEM((tm + SUBLANES, tf), F32)],
        compiler_params=_cparams(("arbitrary", "arbitrary")),
        name=name,
    )(x, h, wg, wv, conv_w, conv_w, w_down, post_w)


def _s_proj_kernel(h_ref, w_ref, y_ref, wb_ref):
    wb = w_ref[...].astype(BF16)
    wb_ref[...] = wb
    y_ref[...] = jnp.dot(h_ref[...], wb, preferred_element_type=F32)


def _s_proj(h, w, col0, ncols, tn, name):
    nb, d = h.shape
    c0 = col0 // tn
    return pl.pallas_call(
        _s_proj_kernel,
        grid=(ncols // tn,),
        in_specs=[pl.BlockSpec((nb, d), lambda j: (0, 0)),
                  pl.BlockSpec((d, tn), lambda j: (0, c0 + j))],
        out_specs=[pl.BlockSpec((nb, tn), lambda j: (0, j)), pl.BlockSpec((d, tn), lambda j: (0, j))],
        out_shape=[jax.ShapeDtypeStruct((nb, ncols), F32), jax.ShapeDtypeStruct((d, ncols), BF16)],
        compiler_params=_cparams(("parallel",)),
        name=name,
    )(h, w)


def _s_conv_proj_kernel(h_ref, wu_ref, wb_ref, wc_ref, z_ref, bg_ref, wu_o, wb_o, wc_o):
    h = h_ref[...]
    wu = wu_ref[...].astype(BF16)
    wb = wb_ref[...].astype(BF16)
    wc = wc_ref[...].astype(BF16)
    wu_o[...] = wu
    wb_o[...] = wb
    wc_o[...] = wc
    u = jnp.dot(h, wu, preferred_element_type=F32)
    cg = jnp.dot(h, wc, preferred_element_type=F32)
    z_ref[...] = cg * u
    bg_ref[...] = jnp.dot(h, wb, preferred_element_type=F32)


def _s_conv_proj(h, w, d_conv, tn, name):
    nb, d = h.shape
    nt = d_conv // tn
    y_spec = pl.BlockSpec((nb, tn), lambda j: (0, j))
    w_out = pl.BlockSpec((d, tn), lambda j: (0, j))
    return pl.pallas_call(
        _s_conv_proj_kernel,
        grid=(nt,),
        in_specs=[pl.BlockSpec((nb, d), lambda j: (0, 0)),
                  pl.BlockSpec((d, tn), lambda j: (0, j)),
                  pl.BlockSpec((d, tn), lambda j: (0, nt + j)),
                  pl.BlockSpec((d, tn), lambda j: (0, 2 * nt + j))],
        out_specs=[y_spec, y_spec, w_out, w_out, w_out],
        out_shape=[jax.ShapeDtypeStruct((nb, d_conv), F32)] * 2
                  + [jax.ShapeDtypeStruct((d, d_conv), BF16)] * 3,
        compiler_params=_cparams(("parallel",)),
        name=name,
    )(h, w, w, w)


def _s_mix_kernel(z_ref, st0_ref, st1_ref, bg_ref, o_ref, ga_ref, gb_ref, cw_ref, wc_ref, wa_ref,
                  m_ref, wc_o, wa_o):
    cw = cw_ref[...]
    zc = st0_ref[...] * cw[0:1] + st1_ref[...] * cw[1:2] + z_ref[...] * cw[2:3]
    wc = wc_ref[...].astype(BF16)
    wa = wa_ref[...].astype(BF16)
    wc_o[...] = wc
    wa_o[...] = wa
    y_conv = jnp.dot((bg_ref[...] * zc).astype(BF16), wc, preferred_element_type=F32)
    y_attn = jnp.dot(o_ref[...], wa, preferred_element_type=F32)
    merged = jax.nn.sigmoid(ga_ref[...]) * y_conv + jax.nn.sigmoid(gb_ref[...]) * y_attn
    m_ref[...] = merged.astype(BF16)


def _s_mix(z, st0, st1, bg, o, gates, conv_w, w_conv_out, w_attn_out, tn, name):
    nb, dc = z.shape
    da, d = w_attn_out.shape
    nt = d // tn
    full = lambda a: pl.BlockSpec(a.shape, lambda j: (0, 0))
    return pl.pallas_call(
        _s_mix_kernel,
        grid=(nt,),
        in_specs=[full(z), full(st0), full(st1), full(bg), full(o),
                  pl.BlockSpec((nb, tn), lambda j: (0, j)),
                  pl.BlockSpec((nb, tn), lambda j: (0, nt + j)),
                  full(conv_w),
                  pl.BlockSpec((dc, tn), lambda j: (0, j)),
                  pl.BlockSpec((da, tn), lambda j: (0, j))],
        out_specs=[pl.BlockSpec((nb, tn), lambda j: (0, j)),
                   pl.BlockSpec((dc, tn), lambda j: (0, j)),
                   pl.BlockSpec((da, tn), lambda j: (0, j))],
        out_shape=[jax.ShapeDtypeStruct((nb, d), BF16),
                   jax.ShapeDtypeStruct(w_conv_out.shape, BF16),
                   jax.ShapeDtypeStruct(w_attn_out.shape, BF16)],
        compiler_params=_cparams(("parallel",)),
        name=name,
    )(z, st0, st1, bg, o, gates, gates, conv_w, w_conv_out, w_attn_out)


def _s_out_kernel(m_ref, wo_ref, x_ref, pw_ref, fw_ref, out_ref, h_ref, wo_o, acc_ref):
    k = pl.program_id(0)

    @pl.when(k == 0)
    def _():
        acc_ref[...] = jnp.zeros_like(acc_ref)

    wo = wo_ref[...].astype(BF16)
    wo_o[...] = wo
    acc_ref[...] += jnp.dot(m_ref[...], wo, preferred_element_type=F32)

    @pl.when(k == pl.num_programs(0) - 1)
    def _():
        x_new = x_ref[...] + _rms(acc_ref[...], pw_ref[...])
        out_ref[...] = x_new
        h_ref[...] = _rms(x_new, fw_ref[...]).astype(BF16)


def _s_out(merged, w_o, x, post_w, ffn_w, tk, name):
    nb, d = x.shape
    full = lambda a: pl.BlockSpec(a.shape, lambda k: (0, 0))
    return pl.pallas_call(
        _s_out_kernel,
        grid=(d // tk,),
        in_specs=[pl.BlockSpec((nb, tk), lambda k: (0, k)),
                  pl.BlockSpec((tk, d), lambda k: (k, 0)),
                  full(x), full(post_w), full(ffn_w)],
        out_specs=[full(x), full(x), pl.BlockSpec((tk, d), lambda k: (k, 0))],
        out_shape=[jax.ShapeDtypeStruct((nb, d), F32), jax.ShapeDtypeStruct((nb, d), BF16),
                   jax.ShapeDtypeStruct(w_o.shape, BF16)],
        scratch_shapes=[pltpu.VMEM((nb, d), F32)],
        compiler_params=_cparams(("arbitrary",)),
        name=name,
    )(merged, w_o, x, post_w, ffn_w)


def _s_ffn_kernel(x_ref, h_ref, wg_ref, wv_ref, cwg_ref, cwv_ref, wd_ref, pw_ref,
                  g0_ref, g1_ref, v0_ref, v1_ref,
                  out_ref, ug_ref, uv_ref, wg_o, wv_o, wd_o, acc_ref):
    j = pl.program_id(0)

    @pl.when(j == 0)
    def _():
        acc_ref[...] = jnp.zeros_like(acc_ref)

    h = h_ref[...]
    wg = wg_ref[...].astype(BF16)
    wv = wv_ref[...].astype(BF16)
    wd = wd_ref[...].astype(BF16)
    wg_o[...] = wg
    wv_o[...] = wv
    wd_o[...] = wd
    up_g = jnp.dot(h, wg, preferred_element_type=F32)
    up_v = jnp.dot(h, wv, preferred_element_type=F32)
    ug_ref[...] = up_g
    uv_ref[...] = up_v
    cwg = cwg_ref[...]
    cwv = cwv_ref[...]
    gate = g0_ref[...] * cwg[0:1] + g1_ref[...] * cwg[1:2] + up_g * cwg[2:3]
    val = v0_ref[...] * cwv[0:1] + v1_ref[...] * cwv[1:2] + up_v * cwv[2:3]
    act = (gate * jax.nn.sigmoid(gate)) * val
    acc_ref[...] += jnp.dot(act.astype(BF16), wd, preferred_element_type=F32)

    @pl.when(j == pl.num_programs(0) - 1)
    def _():
        out_ref[...] = x_ref[...] + _rms(acc_ref[...], pw_ref[...])


def _s_ffn(x, h, w_up, conv_w, w_down, post_w, prev, tf, name):
    nb, d = x.shape
    d_ff = w_down.shape[0]
    nj = d_ff // tf
    full = lambda a: pl.BlockSpec(a.shape, lambda j: (0, 0))
    col = pl.BlockSpec((nb, tf), lambda j: (0, j))
    up_w = pl.BlockSpec((d, tf), lambda j: (0, j))
    down_w = pl.BlockSpec((tf, d), lambda j: (j, 0))
    return pl.pallas_call(
        _s_ffn_kernel,
        grid=(nj,),
        in_specs=[full(x), full(h), up_w, pl.BlockSpec((d, tf), lambda j: (0, nj + j)),
                  pl.BlockSpec((CONV_W, tf), lambda j: (0, j)),
                  pl.BlockSpec((CONV_W, tf), lambda j: (0, nj + j)),
                  down_w, full(post_w), col, col, col, col],
        out_specs=[full(x), col, col, up_w, up_w, down_w],
        out_shape=[jax.ShapeDtypeStruct((nb, d), F32),
                   jax.ShapeDtypeStruct((nb, d_ff), F32), jax.ShapeDtypeStruct((nb, d_ff), F32),
                   jax.ShapeDtypeStruct((d, d_ff), BF16), jax.ShapeDtypeStruct((d, d_ff), BF16),
                   jax.ShapeDtypeStruct((d_ff, d), BF16)],
        scratch_shapes=[pltpu.VMEM((nb, d), F32)],
        compiler_params=_cparams(("arbitrary",)),
        name=name,
    )(x, h, w_up, w_up, conv_w, conv_w, w_down, post_w, *prev)


def kernel(x_prompt, x_sample, cache_k, cache_v, page_table, state_conv, state_ffn_conv,
           pre_mix_w, w_in, conv_w, w_conv_out, lambda_q1, lambda_k1, lambda_q2, lambda_k2,
           subln_w, w_attn_out, w_o, post_mix_w, pre_ffn_w, w_up, ffn_conv_w, w_down, post_ffn_w):
    batch, seq, d = x_prompt.shape
    nb = x_sample.shape[0]
    depth = w_in.shape[0]
    n_heads = cache_k.shape[3]
    kd = cache_k.shape[4]
    d_conv = conv_w.shape[2]
    d_ff = w_down.shape[1]
    rows = batch * seq
    qkv = n_heads * kd
    c_q, c_k, c_v, c_g = 3 * d_conv, 3 * d_conv + qkv, 3 * d_conv + 2 * qkv, 3 * d_conv + 3 * qkv
    slopes = jnp.exp2(-8.0 * jnp.arange(1, n_heads + 1, dtype=F32) / n_heads)

    xp = x_prompt.reshape(rows, d)
    xs = x_sample.reshape(nb, d)
    outs = {n: [] for n in ("kp", "vp", "cp", "fp", "ks", "vs", "cs", "fs")}
    tm, tn = 1024, 1024
    tm_merge, tm_ffn, tf = 256, 512, 512
    ts = 512

    for l in range(depth):
        lam_init = 0.8 - 0.6 * math.exp(-0.3 * l)
        row2 = lambda a: a[l].reshape(1, -1)
        lams = [row2(lambda_q1), row2(lambda_k1), row2(lambda_q2), row2(lambda_k2)]
        pre_w, post_w = row2(pre_mix_w), row2(post_mix_w)
        pre_f, post_f, sub_w = row2(pre_ffn_w), row2(post_ffn_w), row2(subln_w)

        hs = _norm(xs, pre_w, nb)
        zs, bgs, wu_b, wb_b, wcg_b = _s_conv_proj(hs, w_in[l], d_conv, ts, "s_conv_proj")
        qkv_s, wqkv_b = _s_proj(hs, w_in[l], c_q, 3 * qkv, ts, "s_qkv_proj")
        gates_s, wg_b = _s_proj(hs, w_in[l], c_g, 2 * d, ts, "s_gate_proj")
        q_s = qkv_s[:, :qkv].reshape(nb, n_heads, kd)
        k_s = qkv_s[:, qkv:2 * qkv].reshape(nb, n_heads, kd)
        v_s = qkv_s[:, 2 * qkv:].reshape(nb, n_heads, kd)
        o_s = _decode_attention(q_s, k_s, v_s, cache_k, cache_v, page_table, l, slopes, lams,
                                sub_w, lam_init)
        sc = state_conv[l]
        merged_s, wc_b, wa_b = _s_mix(zs, sc[:, 0], sc[:, 1], bgs, o_s.reshape(nb, qkv).astype(BF16),
                                      gates_s, conv_w[l], w_conv_out[l], w_attn_out[l], ts, "s_mix")
        xs, hfs, wo_b = _s_out(merged_s, w_o[l], xs, post_w, pre_f, ts, "s_out")
        sf = state_ffn_conv[l]
        prev = (sf[:, 0, :d_ff], sf[:, 1, :d_ff], sf[:, 0, d_ff:], sf[:, 1, d_ff:])
        xs, ug, uv, wug_b, wuv_b, wd_b = _s_ffn(xs, hfs, w_up[l], ffn_conv_w[l], w_down[l], post_f,
                                                prev, tf, "s_ffn")
        outs["ks"].append(k_s.reshape(nb, 1, n_heads, kd))
        outs["vs"].append(v_s.reshape(nb, 1, n_heads, kd))
        outs["cs"].append(jnp.stack([sc[:, 1], zs], axis=1))
        outs["fs"].append(jnp.stack([sf[:, 1], jnp.concatenate([ug, uv], axis=-1)], axis=1))

        hp = _norm(xp, pre_w, 512)
        z, bg = _conv_proj(hp, wu_b, wb_b, wcg_b, tm, 512, "p_conv_proj")
        q_b = _proj(hp, wqkv_b, 0, qkv, BF16, tm, tn, "p_q_proj")
        k_f, k_b = _kv_proj(hp, wqkv_b, 1, n_heads, kd, 512, "p_k_proj")
        v_f, v_b = _kv_proj(hp, wqkv_b, 2, n_heads, kd, 512, "p_v_proj")
        gates = _proj(hp, wg_b, 0, 2 * d, F32, tm, tn, "p_gate_proj")
        o = _flash_attention(q_b, k_b, v_b, slopes, lams, sub_w, batch, seq, n_heads, lam_init)
        xp, hf = _merge(z, bg, o, gates, xp, conv_w[l], wc_b, wa_b, wo_b, post_w, pre_f,
                        tm_merge, seq, "p_merge")
        xp, fg, fv = _ffn(xp, hf, wug_b, wuv_b, ffn_conv_w[l], wd_b, post_f, tm_ffn, seq, tf, "p_ffn")
        last = seq // tm_ffn - 1
        outs["kp"].append(k_f.reshape(batch, seq, n_heads, kd))
        outs["vp"].append(v_f.reshape(batch, seq, n_heads, kd))
        outs["cp"].append(z.reshape(batch, seq, d_conv)[:, seq - (CONV_W - 1):])
        outs["fp"].append(jnp.concatenate([fg, fv], axis=-1)[last::seq // tm_ffn])

    st = lambda n: jnp.stack(outs[n])
    return (xp.reshape(batch, seq, d), xs.reshape(nb, 1, d),
            st("kp"), st("vp"), st("cp"), st("fp"),
            st("ks"), st("vs"), st("cs"), st("fs"))
```

```python
import functools
import math

import jax
import jax.numpy as jnp
import numpy as np
from jax import lax
from jax.experimental import pallas as pl
from jax.experimental.pallas import tpu as pltpu

F32 = jnp.float32
BF16 = jnp.bfloat16

HEAD_DIM = 128
CONV_W = 3
RMS_EPS = 1e-6
NEG_BIG = -1e30
LOG2E = 1.4426950408889634
LANES = 128
SUBLANES = 8
VMEM_LIMIT = 56 * 1024 * 1024
PARTIAL_SUMS = 1
FFN_SUB = 256


def _bf16_pieces(x, n):
    pieces = []
    for _ in range(n):
        p = float(np.asarray(x, np.float32).astype(BF16))
        pieces.append(p)
        x -= p
    return tuple(pieces)


SQRT_HEAD_DIM_PIECES = _bf16_pieces(HEAD_DIM ** 0.5, 3)


def _cparams(sem):
    return pltpu.CompilerParams(dimension_semantics=sem, vmem_limit_bytes=VMEM_LIMIT)


def _rms(x, w):
    ms = jnp.mean(x * x, axis=-1, keepdims=True)
    return x * lax.rsqrt(ms + RMS_EPS) * w


def _norm_kernel(x_ref, nw_ref, h_ref):
    h_ref[...] = _rms(x_ref[...], nw_ref[...]).astype(BF16)


def _norm(x, nw, tm):
    rows, d = x.shape
    return pl.pallas_call(
        _norm_kernel,
        grid=(rows // tm,),
        in_specs=[pl.BlockSpec((tm, d), lambda i: (i, 0)), pl.BlockSpec((1, d), lambda i: (0, 0))],
        out_specs=pl.BlockSpec((tm, d), lambda i: (i, 0)),
        out_shape=jax.ShapeDtypeStruct((rows, d), BF16),
        compiler_params=_cparams(("parallel",)),
        name="rmsnorm",
    )(x, nw)


def _proj_kernel(h_ref, w_ref, o_ref):
    y = jnp.dot(h_ref[...], w_ref[...], preferred_element_type=F32)
    o_ref[...] = y.astype(o_ref.dtype)


def _proj(h, w, col0, ncols, out_dtype, tm, tn, name):
    rows, d = h.shape
    c0 = col0 // tn
    return pl.pallas_call(
        _proj_kernel,
        grid=(rows // tm, ncols // tn),
        in_specs=[
            pl.BlockSpec((tm, d), lambda i, j: (i, 0)),
            pl.BlockSpec((d, tn), lambda i, j: (0, c0 + j)),
        ],
        out_specs=pl.BlockSpec((tm, tn), lambda i, j: (i, j)),
        out_shape=jax.ShapeDtypeStruct((rows, ncols), out_dtype),
        compiler_params=_cparams(("parallel", "parallel")),
        name=name,
    )(h, w)


def _kv_proj_kernel(h_ref, w_ref, o_ref, ob_ref, *, tm, n_heads):
    y = jnp.dot(h_ref[...], w_ref[...], preferred_element_type=F32)
    ob_ref[...] = y.astype(BF16)
    o_ref[...] = y.reshape(tm, n_heads, y.shape[1] // n_heads)


def _kv_proj(h, w, col_block, n_heads, kd, tm, name):
    rows, d = h.shape
    width = n_heads * kd
    return pl.pallas_call(
        functools.partial(_kv_proj_kernel, tm=tm, n_heads=n_heads),
        grid=(rows // tm,),
        in_specs=[pl.BlockSpec((tm, d), lambda i: (i, 0)),
                  pl.BlockSpec((d, width), lambda i: (0, col_block))],
        out_specs=[pl.BlockSpec((tm, n_heads, kd), lambda i: (i, 0, 0)),
                   pl.BlockSpec((tm, width), lambda i: (i, 0))],
        out_shape=[jax.ShapeDtypeStruct((rows, n_heads, kd), F32),
                   jax.ShapeDtypeStruct((rows, width), BF16)],
        compiler_params=_cparams(("parallel",)),
        name=name,
    )(h, w)


def _conv_proj_kernel(h_ref, wu_ref, wb_ref, wc_ref, z_ref, bg_ref):
    h = h_ref[...]
    u = jnp.dot(h, wu_ref[...], preferred_element_type=F32)
    cg = jnp.dot(h, wc_ref[...], preferred_element_type=F32)
    z_ref[...] = cg * u
    bg_ref[...] = jnp.dot(h, wb_ref[...], preferred_element_type=F32)


def _conv_proj(h, wu, wb, wc, tm, tn, name):
    rows, d = h.shape
    d_conv = wu.shape[1]
    w_spec = pl.BlockSpec((d, tn), lambda i, j: (0, j))
    return pl.pallas_call(
        _conv_proj_kernel,
        grid=(rows // tm, d_conv // tn),
        in_specs=[pl.BlockSpec((tm, d), lambda i, j: (i, 0)), w_spec, w_spec, w_spec],
        out_specs=[pl.BlockSpec((tm, tn), lambda i, j: (i, j))] * 2,
        out_shape=[jax.ShapeDtypeStruct((rows, d_conv), F32)] * 2,
        compiler_params=_cparams(("parallel", "parallel")),
        name=name,
    )(h, wu, wb, wc)


def _lambda_value(lq1_ref, lk1_ref, lq2_ref, lk2_ref, lam_init):
    a = jnp.sum(lq1_ref[...] * lk1_ref[...], axis=-1, keepdims=True)
    b = jnp.sum(lq2_ref[...] * lk2_ref[...], axis=-1, keepdims=True)
    return jnp.exp(a) - jnp.exp(b) + lam_init


def _head_out(o, sub_w, lam_init):
    return _rms(o, sub_w) * (1.0 - lam_init)


def _flash_kernel(slope_ref, q_ref, k_ref, v_ref, lq1_ref, lk1_ref, lq2_ref, lk2_ref,
                  subt_ref, o_ref, acc1_ref, acc2_ref, vt_ref, sa1_ref, sa2_ref, sb1_ref, sb2_ref,
                  *, tq, lam_init):
    h = pl.program_id(1)
    qi = pl.program_id(2)
    slope = slope_ref[h]
    scale2 = HEAD_DIM ** -0.5 * LOG2E

    lane = lax.broadcasted_iota(jnp.int32, (tq, HEAD_DIM), 1)
    pos = lax.broadcasted_iota(jnp.int32, (tq, HEAD_DIM), 0)
    n_pc = len(SQRT_HEAD_DIM_PIECES)
    low = pos & 1
    k_feat = jnp.where(lane < n_pc, pos - low, jnp.where(lane < 2 * n_pc, low, 0))
    aug_k = (k_feat.astype(F32) * slope).astype(BF16)
    q_feat = jnp.zeros((tq, HEAD_DIM), F32)
    for n, piece in enumerate(SQRT_HEAD_DIM_PIECES):
        q_feat = jnp.where((lane == n) | (lane == n + n_pc), piece, q_feat)
    aug_q = q_feat.astype(BF16)
    qa = [jnp.concatenate([q_ref[:, c0:c0 + HEAD_DIM], aug_q], axis=1) for c0 in (0, HEAD_DIM)]

    tk = tq // 2
    slope2 = slope * LOG2E

    @pl.when(qi == 0)
    def _():
        for blk in range(vt_ref.shape[0]):
            vt_ref[blk] = v_ref[blk * tk:(blk + 1) * tk, :].astype(F32).T.astype(BF16)

    acc_refs = (acc1_ref, acc2_ref)
    acc1_ref[...] = jnp.zeros_like(acc1_ref)
    acc2_ref[...] = jnp.zeros_like(acc2_ref)

    def qk_store(kb, dst_refs):
        off = pl.multiple_of(kb * tk, tk)
        k = k_ref[pl.ds(off, tk), :]
        for n in range(2):
            ka = jnp.concatenate([k[:, n * HEAD_DIM:(n + 1) * HEAD_DIM], aug_k[:tk]], axis=1)
            s = lax.dot_general(ka, qa[n], (((1,), (1,)), ((), ())), preferred_element_type=F32)
            dst_refs[n][...] = s * scale2

    def consume(kb, src_refs, carry, mask_shift):
        vt = vt_ref[kb]
        c = slope2 * (kb * tk - qi * tq).astype(F32)
        new_carry, alphas, ps = [], [], []
        for n in range(2):
            m, l = carry[2 * n], carry[2 * n + 1]
            s = src_refs[n][...]
            if mask_shift is not None:
                key = lax.broadcasted_iota(jnp.int32, (tk, tq), 0)
                qry = lax.broadcasted_iota(jnp.int32, (tk, tq), 1)
                s = jnp.where(key + mask_shift <= qry, s, NEG_BIG)
            m_new = jnp.maximum(m, jnp.max(s, axis=0, keepdims=True) + c)
            alpha = jnp.exp2(m - m_new)
            p = jnp.exp2(s - (m_new - c))
            new_carry += [m_new, alpha * l + jnp.sum(p, axis=0, keepdims=True)]
            alphas.append(alpha)
            ps.append(p.astype(BF16))
        for n in range(2):
            acc_refs[n][...] = alphas[n] * acc_refs[n][...] + jnp.dot(
                vt, ps[n], preferred_element_type=F32)
        return tuple(new_carry)

    buf_a = (sa1_ref, sa2_ref)
    buf_b = (sb1_ref, sb2_ref)
    qk_store(0, buf_a)

    def pair(t, carry):
        kb = 2 * t
        qk_store(kb + 1, buf_b)
        carry = consume(kb, buf_a, carry, None)
        qk_store(kb + 2, buf_a)
        return consume(kb + 1, buf_b, carry, None)

    m0 = jnp.full((1, tq), NEG_BIG, F32)
    l0 = jnp.zeros((1, tq), F32)
    carry = lax.fori_loop(0, qi, pair, (m0, l0, m0, l0))
    qk_store(2 * qi + 1, buf_b)
    carry = consume(2 * qi, buf_a, carry, 0)
    m1, l1, m2, l2 = consume(2 * qi + 1, buf_b, carry, tk)

    lam = _lambda_value(lq1_ref, lk1_ref, lq2_ref, lk2_ref, lam_init)
    o_t = acc1_ref[...] / l1 - lam * (acc2_ref[...] / l2)
    ms = jnp.mean(o_t * o_t, axis=0, keepdims=True)
    y_t = o_t * lax.rsqrt(ms + RMS_EPS) * subt_ref[...] * (1.0 - lam_init)
    o_ref[...] = y_t.T.astype(BF16)


def _flash_attention(q, k, v, slopes, lams, sub_w, batch, seq, n_heads, lam_init, tq=512):
    kd = 2 * HEAD_DIM
    nq = seq // tq
    lam_spec = pl.BlockSpec((1, HEAD_DIM), lambda b, h, i: (0, 0))
    return pl.pallas_call(
        functools.partial(_flash_kernel, tq=tq, lam_init=lam_init),
        grid=(batch, n_heads, nq),
        in_specs=[
            pl.BlockSpec(memory_space=pltpu.SMEM),
            pl.BlockSpec((tq, kd), lambda b, h, i: (b * nq + i, h)),
            pl.BlockSpec((seq, kd), lambda b, h, i: (b, h)),
            pl.BlockSpec((seq, kd), lambda b, h, i: (b, h)),
            lam_spec, lam_spec, lam_spec, lam_spec,
            pl.BlockSpec((kd, 1), lambda b, h, i: (0, 0)),
        ],
        out_specs=pl.BlockSpec((tq, kd), lambda b, h, i: (b * nq + i, h)),
        out_shape=jax.ShapeDtypeStruct(q.shape, BF16),
        scratch_shapes=[pltpu.VMEM((kd, tq), F32), pltpu.VMEM((kd, tq), F32),
                        pltpu.VMEM((2 * nq, kd, tq // 2), BF16)]
                       + [pltpu.VMEM((tq // 2, tq), F32)] * 4,
        compiler_params=_cparams(("parallel", "parallel", "arbitrary")),
        name="prompt_attention",
    )(slopes, q, k, v, *lams, sub_w.reshape(kd, 1))


def _decode_kernel(pt_ref, q_ref, kn_ref, vn_ref, lq1_ref, lk1_ref, lq2_ref, lk2_ref,
                   sub_ref, slope_ref, tbl_ref, *rest, pages_per_step, page, n_pages,
                   lam_init, tokens_per_iter):
    del pt_ref
    k_refs = rest[:pages_per_step]
    v_refs = rest[pages_per_step:2 * pages_per_step]
    o_ref, m_ref, l_ref, acc_ref, s_even_ref, s_odd_ref = rest[2 * pages_per_step:]
    s_refs = (s_even_ref, s_odd_ref)
    g = pl.program_id(1)
    past = n_pages * page
    n_iter = page // tokens_per_iter

    @pl.when(g == 0)
    def _():
        m_ref[...] = jnp.full_like(m_ref, NEG_BIG)
        l_ref[...] = jnp.zeros_like(l_ref)
        acc_ref[...] = jnp.zeros_like(acc_ref)

    qv = q_ref[0] * (HEAD_DIM ** -0.5 * LOG2E)
    q1 = qv[:, :HEAD_DIM]
    q2 = qv[:, HEAD_DIM:]
    slope2 = slope_ref[...]

    def rowsum(x):
        return jnp.broadcast_to(jnp.sum(x, axis=-1, keepdims=True), x.shape)

    def twice(p):
        return jnp.concatenate([p, p], axis=-1)

    def run(score_page, acc_page, shifts):
        if score_page is not None:
            k_ref, sa_ref = k_refs[score_page], s_refs[score_page % 2]
        if acc_page is not None:
            v_ref, sb_ref = v_refs[acc_page], s_refs[acc_page % 2]

        def body(it, carry):
            mx1, mx2, parts = carry
            parts = [list(p) for p in parts]
            base = it * tokens_per_iter
            for tt in range(tokens_per_iter):
                t = base + tt
                if score_page is not None:
                    kt = k_ref[t]
                    tb = tbl_ref[t]
                    u1 = rowsum(kt[:, :HEAD_DIM] * q1) + tb
                    u2 = rowsum(kt[:, HEAD_DIM:] * q2) + tb
                    sa_ref[t, 0] = u1
                    sa_ref[t, 1] = u2
                    mx1 = jnp.maximum(mx1, u1)
                    mx2 = jnp.maximum(mx2, u2)
                if acc_page is not None:
                    p1 = jnp.exp2(sb_ref[t, 0] - shifts[0])
                    p2 = jnp.exp2(sb_ref[t, 1] - shifts[1])
                    vt = v_ref[t]
                    part = parts[tt % PARTIAL_SUMS]
                    part[0] = part[0] + p1
                    part[1] = part[1] + p2
                    part[2] = part[2] + twice(p1) * vt
                    part[3] = part[3] + twice(p2) * vt
            return mx1, mx2, tuple(tuple(p) for p in parts)

        neg = jnp.full((SUBLANES, LANES), NEG_BIG, F32)
        zero = (jnp.zeros_like(l_ref[0]), jnp.zeros_like(l_ref[0]),
                jnp.zeros_like(acc_ref[0]), jnp.zeros_like(acc_ref[0]))
        first = (l_ref[0], l_ref[1], acc_ref[0], acc_ref[1])
        if acc_page is None:
            parts0 = ()
        else:
            parts0 = (first,) + (zero,) * (PARTIAL_SUMS - 1)
        mx1, mx2, parts = lax.fori_loop(0, n_iter, body, (neg, neg, parts0))
        if acc_page is not None:
            l_ref[0] = sum(p[0] for p in parts[1:]) + parts[0][0]
            l_ref[1] = sum(p[1] for p in parts[1:]) + parts[0][1]
            acc_ref[0] = sum(p[2] for p in parts[1:]) + parts[0][2]
            acc_ref[1] = sum(p[3] for p in parts[1:]) + parts[0][3]
        return mx1, mx2

    def new_max(c, m_cand):
        m_new = jnp.maximum(m_ref[c], m_cand)
        alpha = jnp.exp2(m_ref[c] - m_new)
        m_ref[c] = m_new
        l_ref[c] = alpha * l_ref[c]
        acc_ref[c] = twice(alpha) * acc_ref[c]
        return m_new

    def page_shifts(r, mx1, mx2):
        first_pos = (g * pages_per_step + r) * page
        c = slope2 * (first_pos - past).astype(F32)
        return new_max(0, mx1 + c) - c, new_max(1, mx2 + c) - c

    mx = run(0, None, None)
    for r in range(pages_per_step):
        shifts = page_shifts(r, *mx)
        if r + 1 < pages_per_step:
            mx = run(r + 1, r, shifts)
        else:
            run(None, r, shifts)

    @pl.when(g == pl.num_programs(1) - 1)
    def _():
        kn = kn_ref[0]
        vn = vn_ref[0]
        s1 = rowsum(kn[:, :HEAD_DIM] * q1)
        s2 = rowsum(kn[:, HEAD_DIM:] * q2)
        p1 = jnp.exp2(s1 - new_max(0, s1))
        p2 = jnp.exp2(s2 - new_max(1, s2))
        o1 = (acc_ref[0] + twice(p1) * vn) / twice(l_ref[0] + p1)
        o2 = (acc_ref[1] + twice(p2) * vn) / twice(l_ref[1] + p2)
        lam = _lambda_value(lq1_ref, lk1_ref, lq2_ref, lk2_ref, lam_init)
        o_ref[0] = _head_out(o1 - lam * o2, sub_ref[...], lam_init)


def _decode_attention(q, k_new, v_new, cache_k, cache_v, page_table, layer, slopes, lams,
                      sub_w, lam_init, pages_per_step=8, tokens_per_iter=128):
    nb, n_heads, kd = q.shape
    n_pages = page_table.shape[1]
    page = cache_k.shape[2]
    steps = n_pages // pages_per_step
    slope2 = jnp.broadcast_to((slopes * LOG2E)[:, None], (n_heads, LANES))
    tok_bias = jnp.arange(page, dtype=F32)[:, None, None] * slope2[None]

    tok_spec = pl.BlockSpec((1, n_heads, kd), lambda b, g, pt: (b, 0, 0))
    lam_spec = pl.BlockSpec((1, HEAD_DIM), lambda b, g, pt: (0, 0))

    def page_spec(r):
        return pl.BlockSpec(
            (None, None, page, n_heads, kd),
            lambda b, g, pt: (layer, pt[b, g * pages_per_step + r], 0, 0, 0))

    page_specs = [page_spec(r) for r in range(pages_per_step)]
    grid_spec = pltpu.PrefetchScalarGridSpec(
        num_scalar_prefetch=1,
        grid=(nb, steps),
        in_specs=[tok_spec, tok_spec, tok_spec, lam_spec, lam_spec, lam_spec, lam_spec,
                  pl.BlockSpec((1, kd), lambda b, g, pt: (0, 0)),
                  pl.BlockSpec((n_heads, LANES), lambda b, g, pt: (0, 0)),
                  pl.BlockSpec((page, n_heads, LANES), lambda b, g, pt: (0, 0, 0))]
                 + page_specs + page_specs,
        out_specs=tok_spec,
        scratch_shapes=[
            pltpu.VMEM((2, n_heads, LANES), F32),
            pltpu.VMEM((2, n_heads, LANES), F32),
            pltpu.VMEM((2, n_heads, kd), F32),
            pltpu.VMEM((page, 2, n_heads, LANES), F32),
            pltpu.VMEM((page, 2, n_heads, LANES), F32),
        ],
    )
    return pl.pallas_call(
        functools.partial(_decode_kernel, pages_per_step=pages_per_step, page=page,
                          n_pages=n_pages, lam_init=lam_init, tokens_per_iter=tokens_per_iter),
        grid_spec=grid_spec,
        out_shape=jax.ShapeDtypeStruct((nb, n_heads, kd), F32),
        compiler_params=_cparams(("parallel", "arbitrary")),
        name="sample_attention",
    )(page_table, q, k_new, v_new, *lams, sub_w, slope2, tok_bias,
      *([cache_k] * pages_per_step), *([cache_v] * pages_per_step))


def _merge_kernel(z_ref, halo_ref, bg_ref, o_ref, ga_ref, gb_ref, x_ref, cw_ref, wc_ref, wa_ref,
                  wo_ref, pw_ref, fw_ref, out_ref, h_ref, zbuf_ref, *, tm, tiles_per_seq):
    i = pl.program_id(0)
    z = z_ref[...]
    starts_seq = i % tiles_per_seq == 0
    zbuf_ref[0:SUBLANES, :] = jnp.where(starts_seq, 0.0, halo_ref[...])
    zbuf_ref[SUBLANES:, :] = z
    z1 = zbuf_ref[SUBLANES - 1:SUBLANES - 1 + tm, :]
    z2 = zbuf_ref[SUBLANES - 2:SUBLANES - 2 + tm, :]
    cw = cw_ref[...]
    zc = z2 * cw[0:1] + z1 * cw[1:2] + z * cw[2:3]
    y_conv = jnp.dot((bg_ref[...] * zc).astype(BF16), wc_ref[...], preferred_element_type=F32)
    y_attn = jnp.dot(o_ref[...], wa_ref[...], preferred_element_type=F32)
    merged = jax.nn.sigmoid(ga_ref[...]) * y_conv + jax.nn.sigmoid(gb_ref[...]) * y_attn
    mo = jnp.dot(merged.astype(BF16), wo_ref[...], preferred_element_type=F32)
    x_new = x_ref[...] + _rms(mo, pw_ref[...])
    out_ref[...] = x_new
    h_ref[...] = _rms(x_new, fw_ref[...]).astype(BF16)


def _merge(z, bg, o, gates, x, conv_w, wc, wa, wo, post_w, ffn_w, tm, seq, name):
    rows, d = x.shape
    dc = z.shape[1]
    row = lambda w: pl.BlockSpec((tm, w), lambda i: (i, 0))
    const = lambda a: pl.BlockSpec(a.shape, lambda i: (0, 0), pipeline_mode=pl.Buffered(1))
    hb = tm // SUBLANES
    halo = pl.BlockSpec((SUBLANES, dc), lambda i: (jnp.maximum(i * hb - 1, 0), 0))
    return pl.pallas_call(
        functools.partial(_merge_kernel, tm=tm, tiles_per_seq=seq // tm),
        grid=(rows // tm,),
        in_specs=[row(dc), halo, row(dc), row(d),
                  pl.BlockSpec((tm, d), lambda i: (i, 0)),
                  pl.BlockSpec((tm, d), lambda i: (i, 1)),
                  row(d), const(conv_w), const(wc), const(wa), const(wo), const(post_w),
                  const(ffn_w)],
        out_specs=[row(d), row(d)],
        out_shape=[jax.ShapeDtypeStruct((rows, d), F32), jax.ShapeDtypeStruct((rows, d), BF16)],
        scratch_shapes=[pltpu.VMEM((tm + SUBLANES, dc), F32)],
        compiler_params=_cparams(("arbitrary",)),
        name=name,
    )(z, z, bg, o, gates, gates, x, conv_w, wc, wa, wo, post_w, ffn_w)


def _ffn_kernel(x_ref, h_ref, wg_ref, wv_ref, cwg_ref, cwv_ref, wd_ref, pw_ref,
                out_ref, sg_ref, sv_ref, acc_ref, cg_ref, cv_ref, ug_ref, uv_ref,
                *, tm, tiles_per_seq):
    i = pl.program_id(0)
    j = pl.program_id(1)

    @pl.when(j == 0)
    def _():
        acc_ref[...] = jnp.zeros_like(acc_ref)

        @pl.when(i % tiles_per_seq == 0)
        def _():
            cg_ref[...] = jnp.zeros_like(cg_ref)
            cv_ref[...] = jnp.zeros_like(cv_ref)

    h = h_ref[...]
    tf = wd_ref.shape[0]

    def up_proj(cs, w_ref, carry_ref, ubuf_ref, state_ref):
        up = jnp.dot(h, w_ref[:, cs], preferred_element_type=F32)
        ubuf_ref[0:SUBLANES, cs] = carry_ref[j, :, cs]
        ubuf_ref[SUBLANES:, cs] = up
        carry_ref[j, :, cs] = up[tm - SUBLANES:, :]
        state_ref[:, cs] = up[tm - (CONV_W - 1):, :]

    def conv(cs, cw_ref, ubuf_ref):
        cw = cw_ref[:, cs]
        u0 = ubuf_ref[SUBLANES:SUBLANES + tm, cs]
        u1 = ubuf_ref[SUBLANES - 1:SUBLANES - 1 + tm, cs]
        u2 = ubuf_ref[SUBLANES - 2:SUBLANES - 2 + tm, cs]
        return u2 * cw[0:1] + u1 * cw[1:2] + u0 * cw[2:3]

    chunks = [slice(c0, c0 + FFN_SUB) for c0 in range(0, tf, FFN_SUB)]
    for cs in chunks:
        up_proj(cs, wg_ref, cg_ref, ug_ref, sg_ref)
        up_proj(cs, wv_ref, cv_ref, uv_ref, sv_ref)
    down = None
    for cs in chunks:
        gate = conv(cs, cwg_ref, ug_ref)
        val = conv(cs, cwv_ref, uv_ref)
        act = (gate * jax.nn.sigmoid(gate)) * val
        part = jnp.dot(act.astype(BF16), wd_ref[cs, :], preferred_element_type=F32)
        down = part if down is None else down + part
    acc_ref[...] += down

    @pl.when(j == pl.num_programs(1) - 1)
    def _():
        out_ref[...] = x_ref[...] + _rms(acc_ref[...], pw_ref[...])


def _ffn(x, h, wg, wv, conv_w, w_down, post_w, tm, seq, tf, name):
    rows, d = x.shape
    d_ff = w_down.shape[0]
    nj = d_ff // tf
    st_spec = pl.BlockSpec((None, CONV_W - 1, tf), lambda i, j: (i, 0, j))
    st_shape = jax.ShapeDtypeStruct((rows // tm, CONV_W - 1, d_ff), F32)
    return pl.pallas_call(
        functools.partial(_ffn_kernel, tm=tm, tiles_per_seq=seq // tm),
        grid=(rows // tm, nj),
        in_specs=[
            pl.BlockSpec((tm, d), lambda i, j: (i, 0)),
            pl.BlockSpec((tm, d), lambda i, j: (i, 0)),
            pl.BlockSpec((d, tf), lambda i, j: (0, j)),
            pl.BlockSpec((d, tf), lambda i, j: (0, j)),
            pl.BlockSpec((CONV_W, tf), lambda i, j: (0, j)),
            pl.BlockSpec((CONV_W, tf), lambda i, j: (0, nj + j)),
            pl.BlockSpec((tf, d), lambda i, j: (j, 0)),
            pl.BlockSpec((1, d), lambda i, j: (0, 0)),
        ],
        out_specs=[pl.BlockSpec((tm, d), lambda i, j: (i, 0)), st_spec, st_spec],
        out_shape=[jax.ShapeDtypeStruct((rows, d), F32), st_shape, st_shape],
        scratch_shapes=[
            pltpu.VMEM((tm, d), F32),
            pltpu.VMEM((nj, SUBLANES, tf), F32), pltpu.VMEM((nj, SUBLANES, tf), F32),
            pltpu.VMEM((tm + SUBLANES, tf), F32), pltpu.VMEM((tm + SUBLANES, tf), F32)],
        compiler_params=_cparams(("arbitrary", "arbitrary")),
        name=name,
    )(x, h, wg, wv, conv_w, conv_w, w_down, post_w)


def _s_proj_kernel(h_ref, w_ref, y_ref, wb_ref):
    wb = w_ref[...].astype(BF16)
    wb_ref[...] = wb
    y_ref[...] = jnp.dot(h_ref[...], wb, preferred_element_type=F32)


def _s_proj(h, w, col0, ncols, tn, name):
    nb, d = h.shape
    c0 = col0 // tn
    return pl.pallas_call(
        _s_proj_kernel,
        grid=(ncols // tn,),
        in_specs=[pl.BlockSpec((nb, d), lambda j: (0, 0)),
                  pl.BlockSpec((d, tn), lambda j: (0, c0 + j))],
        out_specs=[pl.BlockSpec((nb, tn), lambda j: (0, j)), pl.BlockSpec((d, tn), lambda j: (0, j))],
        out_shape=[jax.ShapeDtypeStruct((nb, ncols), F32), jax.ShapeDtypeStruct((d, ncols), BF16)],
        compiler_params=_cparams(("parallel",)),
        name=name,
    )(h, w)


def _s_conv_proj_kernel(h_ref, wu_ref, wb_ref, wc_ref, z_ref, bg_ref, wu_o, wb_o, wc_o):
    h = h_ref[...]
    wu = wu_ref[...].astype(BF16)
    wb = wb_ref[...].astype(BF16)
    wc = wc_ref[...].astype(BF16)
    wu_o[...] = wu
    wb_o[...] = wb
    wc_o[...] = wc
    u = jnp.dot(h, wu, preferred_element_type=F32)
    cg = jnp.dot(h, wc, preferred_element_type=F32)
    z_ref[...] = cg * u
    bg_ref[...] = jnp.dot(h, wb, preferred_element_type=F32)


def _s_conv_proj(h, w, d_conv, tn, name):
    nb, d = h.shape
    nt = d_conv // tn
    y_spec = pl.BlockSpec((nb, tn), lambda j: (0, j))
    w_out = pl.BlockSpec((d, tn), lambda j: (0, j))
    return pl.pallas_call(
        _s_conv_proj_kernel,
        grid=(nt,),
        in_specs=[pl.BlockSpec((nb, d), lambda j: (0, 0)),
                  pl.BlockSpec((d, tn), lambda j: (0, j)),
                  pl.BlockSpec((d, tn), lambda j: (0, nt + j)),
                  pl.BlockSpec((d, tn), lambda j: (0, 2 * nt + j))],
        out_specs=[y_spec, y_spec, w_out, w_out, w_out],
        out_shape=[jax.ShapeDtypeStruct((nb, d_conv), F32)] * 2
                  + [jax.ShapeDtypeStruct((d, d_conv), BF16)] * 3,
        compiler_params=_cparams(("parallel",)),
        name=name,
    )(h, w, w, w)


def _s_mix_kernel(z_ref, st0_ref, st1_ref, bg_ref, o_ref, ga_ref, gb_ref, cw_ref, wc_ref, wa_ref,
                  m_ref, wc_o, wa_o):
    cw = cw_ref[...]
    zc = st0_ref[...] * cw[0:1] + st1_ref[...] * cw[1:2] + z_ref[...] * cw[2:3]
    wc = wc_ref[...].astype(BF16)
    wa = wa_ref[...].astype(BF16)
    wc_o[...] = wc
    wa_o[...] = wa
    y_conv = jnp.dot((bg_ref[...] * zc).astype(BF16), wc, preferred_element_type=F32)
    y_attn = jnp.dot(o_ref[...], wa, preferred_element_type=F32)
    merged = jax.nn.sigmoid(ga_ref[...]) * y_conv + jax.nn.sigmoid(gb_ref[...]) * y_attn
    m_ref[...] = merged.astype(BF16)


def _s_mix(z, st0, st1, bg, o, gates, conv_w, w_conv_out, w_attn_out, tn, name):
    nb, dc = z.shape
    da, d = w_attn_out.shape
    nt = d // tn
    full = lambda a: pl.BlockSpec(a.shape, lambda j: (0, 0))
    return pl.pallas_call(
        _s_mix_kernel,
        grid=(nt,),
        in_specs=[full(z), full(st0), full(st1), full(bg), full(o),
                  pl.BlockSpec((nb, tn), lambda j: (0, j)),
                  pl.BlockSpec((nb, tn), lambda j: (0, nt + j)),
                  full(conv_w),
                  pl.BlockSpec((dc, tn), lambda j: (0, j)),
                  pl.BlockSpec((da, tn), lambda j: (0, j))],
        out_specs=[pl.BlockSpec((nb, tn), lambda j: (0, j)),
                   pl.BlockSpec((dc, tn), lambda j: (0, j)),
                   pl.BlockSpec((da, tn), lambda j: (0, j))],
        out_shape=[jax.ShapeDtypeStruct((nb, d), BF16),
                   jax.ShapeDtypeStruct(w_conv_out.shape, BF16),
                   jax.ShapeDtypeStruct(w_attn_out.shape, BF16)],
        compiler_params=_cparams(("parallel",)),
        name=name,
    )(z, st0, st1, bg, o, gates, gates, conv_w, w_conv_out, w_attn_out)


def _s_out_kernel(m_ref, wo_ref, x_ref, pw_ref, fw_ref, out_ref, h_ref, wo_o, acc_ref):
    k = pl.program_id(0)

    @pl.when(k == 0)
    def _():
        acc_ref[...] = jnp.zeros_like(acc_ref)

    wo = wo_ref[...].astype(BF16)
    wo_o[...] = wo
    acc_ref[...] += jnp.dot(m_ref[...], wo, preferred_element_type=F32)

    @pl.when(k == pl.num_programs(0) - 1)
    def _():
        x_new = x_ref[...] + _rms(acc_ref[...], pw_ref[...])
        out_ref[...] = x_new
        h_ref[...] = _rms(x_new, fw_ref[...]).astype(BF16)


def _s_out(merged, w_o, x, post_w, ffn_w, tk, name):
    nb, d = x.shape
    full = lambda a: pl.BlockSpec(a.shape, lambda k: (0, 0))
    return pl.pallas_call(
        _s_out_kernel,
        grid=(d // tk,),
        in_specs=[pl.BlockSpec((nb, tk), lambda k: (0, k)),
                  pl.BlockSpec((tk, d), lambda k: (k, 0)),
                  full(x), full(post_w), full(ffn_w)],
        out_specs=[full(x), full(x), pl.BlockSpec((tk, d), lambda k: (k, 0))],
        out_shape=[jax.ShapeDtypeStruct((nb, d), F32), jax.ShapeDtypeStruct((nb, d), BF16),
                   jax.ShapeDtypeStruct(w_o.shape, BF16)],
        scratch_shapes=[pltpu.VMEM((nb, d), F32)],
        compiler_params=_cparams(("arbitrary",)),
        name=name,
    )(merged, w_o, x, post_w, ffn_w)


def _s_ffn_kernel(x_ref, h_ref, wg_ref, wv_ref, cwg_ref, cwv_ref, wd_ref, pw_ref,
                  g0_ref, g1_ref, v0_ref, v1_ref,
                  out_ref, ug_ref, uv_ref, wg_o, wv_o, wd_o, acc_ref):
    j = pl.program_id(0)

    @pl.when(j == 0)
    def _():
        acc_ref[...] = jnp.zeros_like(acc_ref)

    h = h_ref[...]
    wg = wg_ref[...].astype(BF16)
    wv = wv_ref[...].astype(BF16)
    wd = wd_ref[...].astype(BF16)
    wg_o[...] = wg
    wv_o[...] = wv
    wd_o[...] = wd
    up_g = jnp.dot(h, wg, preferred_element_type=F32)
    up_v = jnp.dot(h, wv, preferred_element_type=F32)
    ug_ref[...] = up_g
    uv_ref[...] = up_v
    cwg = cwg_ref[...]
    cwv = cwv_ref[...]
    gate = g0_ref[...] * cwg[0:1] + g1_ref[...] * cwg[1:2] + up_g * cwg[2:3]
    val = v0_ref[...] * cwv[0:1] + v1_ref[...] * cwv[1:2] + up_v * cwv[2:3]
    act = (gate * jax.nn.sigmoid(gate)) * val
    acc_ref[...] += jnp.dot(act.astype(BF16), wd, preferred_element_type=F32)

    @pl.when(j == pl.num_programs(0) - 1)
    def _():
        out_ref[...] = x_ref[...] + _rms(acc_ref[...], pw_ref[...])


def _s_ffn(x, h, w_up, conv_w, w_down, post_w, prev, tf, name):
    nb, d = x.shape
    d_ff = w_down.shape[0]
    nj = d_ff // tf
    full = lambda a: pl.BlockSpec(a.shape, lambda j: (0, 0))
    col = pl.BlockSpec((nb, tf), lambda j: (0, j))
    up_w = pl.BlockSpec((d, tf), lambda j: (0, j))
    down_w = pl.BlockSpec((tf, d), lambda j: (j, 0))
    return pl.pallas_call(
        _s_ffn_kernel,
        grid=(nj,),
        in_specs=[full(x), full(h), up_w, pl.BlockSpec((d, tf), lambda j: (0, nj + j)),
                  pl.BlockSpec((CONV_W, tf), lambda j: (0, j)),
                  pl.BlockSpec((CONV_W, tf), lambda j: (0, nj + j)),
                  down_w, full(post_w), col, col, col, col],
        out_specs=[full(x), col, col, up_w, up_w, down_w],
        out_shape=[jax.ShapeDtypeStruct((nb, d), F32),
                   jax.ShapeDtypeStruct((nb, d_ff), F32), jax.ShapeDtypeStruct((nb, d_ff), F32),
                   jax.ShapeDtypeStruct((d, d_ff), BF16), jax.ShapeDtypeStruct((d, d_ff), BF16),
                   jax.ShapeDtypeStruct((d_ff, d), BF16)],
        scratch_shapes=[pltpu.VMEM((nb, d), F32)],
        compiler_params=_cparams(("arbitrary",)),
        name=name,
    )(x, h, w_up, w_up, conv_w, conv_w, w_down, post_w, *prev)


def kernel(x_prompt, x_sample, cache_k, cache_v, page_table, state_conv, state_ffn_conv,
           pre_mix_w, w_in, conv_w, w_conv_out, lambda_q1, lambda_k1, lambda_q2, lambda_k2,
           subln_w, w_attn_out, w_o, post_mix_w, pre_ffn_w, w_up, ffn_conv_w, w_down, post_ffn_w):
    batch, seq, d = x_prompt.shape
    nb = x_sample.shape[0]
    depth = w_in.shape[0]
    n_heads = cache_k.shape[3]
    kd = cache_k.shape[4]
    d_conv = conv_w.shape[2]
    d_ff = w_down.shape[1]
    rows = batch * seq
    qkv = n_heads * kd
    c_q, c_k, c_v, c_g = 3 * d_conv, 3 * d_conv + qkv, 3 * d_conv + 2 * qkv, 3 * d_conv + 3 * qkv
    slopes = jnp.exp2(-8.0 * jnp.arange(1, n_heads + 1, dtype=F32) / n_heads)

    xp = x_prompt.reshape(rows, d)
    xs = x_sample.reshape(nb, d)
    outs = {n: [] for n in ("kp", "vp", "cp", "fp", "ks", "vs", "cs", "fs")}
    tm, tn = 1024, 1024
    tm_merge, tm_ffn, tf = 256, 512, 512
    ts = 512

    for l in range(depth):
        lam_init = 0.8 - 0.6 * math.exp(-0.3 * l)
        row2 = lambda a: a[l].reshape(1, -1)
        lams = [row2(lambda_q1), row2(lambda_k1), row2(lambda_q2), row2(lambda_k2)]
        pre_w, post_w = row2(pre_mix_w), row2(post_mix_w)
        pre_f, post_f, sub_w = row2(pre_ffn_w), row2(post_ffn_w), row2(subln_w)

        hs = _norm(xs, pre_w, nb)
        zs, bgs, wu_b, wb_b, wcg_b = _s_conv_proj(hs, w_in[l], d_conv, ts, "s_conv_proj")
        qkv_s, wqkv_b = _s_proj(hs, w_in[l], c_q, 3 * qkv, ts, "s_qkv_proj")
        gates_s, wg_b = _s_proj(hs, w_in[l], c_g, 2 * d, ts, "s_gate_proj")
        q_s = qkv_s[:, :qkv].reshape(nb, n_heads, kd)
        k_s = qkv_s[:, qkv:2 * qkv].reshape(nb, n_heads, kd)
        v_s = qkv_s[:, 2 * qkv:].reshape(nb, n_heads, kd)
        o_s = _decode_attention(q_s, k_s, v_s, cache_k, cache_v, page_table, l, slopes, lams,
                                sub_w, lam_init)
        sc = state_conv[l]
        merged_s, wc_b, wa_b = _s_mix(zs, sc[:, 0], sc[:, 1], bgs, o_s.reshape(nb, qkv).astype(BF16),
                                      gates_s, conv_w[l], w_conv_out[l], w_attn_out[l], ts, "s_mix")
        xs, hfs, wo_b = _s_out(merged_s, w_o[l], xs, post_w, pre_f, ts, "s_out")
        sf = state_ffn_conv[l]
        prev = (sf[:, 0, :d_ff], sf[:, 1, :d_ff], sf[:, 0, d_ff:], sf[:, 1, d_ff:])
        xs, ug, uv, wug_b, wuv_b, wd_b = _s_ffn(xs, hfs, w_up[l], ffn_conv_w[l], w_down[l], post_f,
                                                prev, tf, "s_ffn")
        outs["ks"].append(k_s.reshape(nb, 1, n_heads, kd))
        outs["vs"].append(v_s.reshape(nb, 1, n_heads, kd))
        outs["cs"].append(jnp.stack([sc[:, 1], zs], axis=1))
        outs["fs"].append(jnp.stack([sf[:, 1], jnp.concatenate([ug, uv], axis=-1)], axis=1))

        hp = _norm(xp, pre_w, 512)
        z, bg = _conv_proj(hp, wu_b, wb_b, wcg_b, tm, 512, "p_conv_proj")
        q_b = _proj(hp, wqkv_b, 0, qkv, BF16, tm, tn, "p_q_proj")
        k_f, k_b = _kv_proj(hp, wqkv_b, 1, n_heads, kd, 512, "p_k_proj")
        v_f, v_b = _kv_proj(hp, wqkv_b, 2, n_heads, kd, 512, "p_v_proj")
        gates = _proj(hp, wg_b, 0, 2 * d, F32, tm, tn, "p_gate_proj")
        o = _flash_attention(q_b, k_b, v_b, slopes, lams, sub_w, batch, seq, n_heads, lam_init)
        xp, hf = _merge(z, bg, o, gates, xp, conv_w[l], wc_b, wa_b, wo_b, post_w, pre_f,
                        tm_merge, seq, "p_merge")
        xp, fg, fv = _ffn(xp, hf, wug_b, wuv_b, ffn_conv_w[l], wd_b, post_f, tm_ffn, seq, tf, "p_ffn")
        last = seq // tm_ffn - 1
        outs["kp"].append(k_f.reshape(batch, seq, n_heads, kd))
        outs["vp"].append(v_f.reshape(batch, seq, n_heads, kd))
        outs["cp"].append(z.reshape(batch, seq, d_conv)[:, seq - (CONV_W - 1):])
        outs["fp"].append(jnp.concatenate([fg, fv], axis=-1)[last::seq // tm_ffn])

    st = lambda n: jnp.stack(outs[n])
    return (xp.reshape(batch, seq, d), xs.reshape(nb, 1, d),
            st("kp"), st("vp"), st("cp"), st("fp"),
            st("ks"), st("vs"), st("cs"), st("fs"))
```

```python
import functools
import math

import jax
import jax.numpy as jnp
import numpy as np
from jax import lax
from jax.experimental import pallas as pl
from jax.experimental.pallas import tpu as pltpu

F32 = jnp.float32
BF16 = jnp.bfloat16

HEAD_DIM = 128
CONV_W = 3
RMS_EPS = 1e-6
NEG_BIG = -1e30
LOG2E = 1.4426950408889634
LANES = 128
SUBLANES = 8
VMEM_LIMIT = 56 * 1024 * 1024
PARTIAL_SUMS = 1
FFN_SUB = 256


def _bf16_pieces(x, n):
    pieces = []
    for _ in range(n):
        p = float(np.asarray(x, np.float32).astype(BF16))
        pieces.append(p)
        x -= p
    return tuple(pieces)


SQRT_HEAD_DIM_PIECES = _bf16_pieces(HEAD_DIM ** 0.5, 3)


def _cparams(sem):
    return pltpu.CompilerParams(dimension_semantics=sem, vmem_limit_bytes=VMEM_LIMIT)


def _rms(x, w):
    ms = jnp.mean(x * x, axis=-1, keepdims=True)
    return x * lax.rsqrt(ms + RMS_EPS) * w


def _norm_kernel(x_ref, nw_ref, h_ref):
    h_ref[...] = _rms(x_ref[...], nw_ref[...]).astype(BF16)


def _norm(x, nw, tm):
    rows, d = x.shape
    return pl.pallas_call(
        _norm_kernel,
        grid=(rows // tm,),
        in_specs=[pl.BlockSpec((tm, d), lambda i: (i, 0)), pl.BlockSpec((1, d), lambda i: (0, 0))],
        out_specs=pl.BlockSpec((tm, d), lambda i: (i, 0)),
        out_shape=jax.ShapeDtypeStruct((rows, d), BF16),
        compiler_params=_cparams(("parallel",)),
        name="rmsnorm",
    )(x, nw)


def _proj_kernel(h_ref, w_ref, o_ref):
    y = jnp.dot(h_ref[...], w_ref[...], preferred_element_type=F32)
    o_ref[...] = y.astype(o_ref.dtype)


def _proj(h, w, col0, ncols, out_dtype, tm, tn, name):
    rows, d = h.shape
    c0 = col0 // tn
    return pl.pallas_call(
        _proj_kernel,
        grid=(rows // tm, ncols // tn),
        in_specs=[
            pl.BlockSpec((tm, d), lambda i, j: (i, 0)),
            pl.BlockSpec((d, tn), lambda i, j: (0, c0 + j)),
        ],
        out_specs=pl.BlockSpec((tm, tn), lambda i, j: (i, j)),
        out_shape=jax.ShapeDtypeStruct((rows, ncols), out_dtype),
        compiler_params=_cparams(("parallel", "parallel")),
        name=name,
    )(h, w)


def _kv_proj_kernel(h_ref, w_ref, o_ref, ob_ref, *, tm, n_heads):
    y = jnp.dot(h_ref[...], w_ref[...], preferred_element_type=F32)
    ob_ref[...] = y.astype(BF16)
    o_ref[...] = y.reshape(tm, n_heads, y.shape[1] // n_heads)


def _kv_proj(h, w, col_block, n_heads, kd, tm, name):
    rows, d = h.shape
    width = n_heads * kd
    return pl.pallas_call(
        functools.partial(_kv_proj_kernel, tm=tm, n_heads=n_heads),
        grid=(rows // tm,),
        in_specs=[pl.BlockSpec((tm, d), lambda i: (i, 0)),
                  pl.BlockSpec((d, width), lambda i: (0, col_block))],
        out_specs=[pl.BlockSpec((tm, n_heads, kd), lambda i: (i, 0, 0)),
                   pl.BlockSpec((tm, width), lambda i: (i, 0))],
        out_shape=[jax.ShapeDtypeStruct((rows, n_heads, kd), F32),
                   jax.ShapeDtypeStruct((rows, width), BF16)],
        compiler_params=_cparams(("parallel",)),
        name=name,
    )(h, w)


def _conv_proj_kernel(h_ref, wu_ref, wb_ref, wc_ref, z_ref, bg_ref):
    h = h_ref[...]
    u = jnp.dot(h, wu_ref[...], preferred_element_type=F32)
    cg = jnp.dot(h, wc_ref[...], preferred_element_type=F32)
    z_ref[...] = cg * u
    bg_ref[...] = jnp.dot(h, wb_ref[...], preferred_element_type=F32)


def _conv_proj(h, wu, wb, wc, tm, tn, name):
    rows, d = h.shape
    d_conv = wu.shape[1]
    w_spec = pl.BlockSpec((d, tn), lambda i, j: (0, j))
    return pl.pallas_call(
        _conv_proj_kernel,
        grid=(rows // tm, d_conv // tn),
        in_specs=[pl.BlockSpec((tm, d), lambda i, j: (i, 0)), w_spec, w_spec, w_spec],
        out_specs=[pl.BlockSpec((tm, tn), lambda i, j: (i, j))] * 2,
        out_shape=[jax.ShapeDtypeStruct((rows, d_conv), F32)] * 2,
        compiler_params=_cparams(("parallel", "parallel")),
        name=name,
    )(h, wu, wb, wc)


def _lambda_value(lq1_ref, lk1_ref, lq2_ref, lk2_ref, lam_init):
    a = jnp.sum(lq1_ref[...] * lk1_ref[...], axis=-1, keepdims=True)
    b = jnp.sum(lq2_ref[...] * lk2_ref[...], axis=-1, keepdims=True)
    return jnp.exp(a) - jnp.exp(b) + lam_init


def _head_out(o, sub_w, lam_init):
    return _rms(o, sub_w) * (1.0 - lam_init)


def _flash_kernel(slope_ref, q_ref, k_ref, v_ref, lq1_ref, lk1_ref, lq2_ref, lk2_ref,
                  subt_ref, o_ref, acc1_ref, acc2_ref, vt_ref, sa1_ref, sa2_ref, sb1_ref, sb2_ref,
                  *, tq, lam_init):
    h = pl.program_id(1)
    qi = pl.program_id(2)
    slope = slope_ref[h]
    scale2 = HEAD_DIM ** -0.5 * LOG2E

    lane = lax.broadcasted_iota(jnp.int32, (tq, HEAD_DIM), 1)
    pos = lax.broadcasted_iota(jnp.int32, (tq, HEAD_DIM), 0)
    n_pc = len(SQRT_HEAD_DIM_PIECES)
    low = pos & 1
    k_feat = jnp.where(lane < n_pc, pos - low, jnp.where(lane < 2 * n_pc, low, 0))
    aug_k = (k_feat.astype(F32) * slope).astype(BF16)
    q_feat = jnp.zeros((tq, HEAD_DIM), F32)
    for n, piece in enumerate(SQRT_HEAD_DIM_PIECES):
        q_feat = jnp.where((lane == n) | (lane == n + n_pc), piece, q_feat)
    aug_q = q_feat.astype(BF16)
    qa = [jnp.concatenate([q_ref[:, c0:c0 + HEAD_DIM], aug_q], axis=1) for c0 in (0, HEAD_DIM)]

    tk = tq // 2
    slope2 = slope * LOG2E

    @pl.when(qi == 0)
    def _():
        for blk in range(vt_ref.shape[0]):
            vt_ref[blk] = v_ref[blk * tk:(blk + 1) * tk, :].astype(F32).T.astype(BF16)

    acc_refs = (acc1_ref, acc2_ref)
    acc1_ref[...] = jnp.zeros_like(acc1_ref)
    acc2_ref[...] = jnp.zeros_like(acc2_ref)

    def qk_store(kb, dst_refs):
        off = pl.multiple_of(kb * tk, tk)
        k = k_ref[pl.ds(off, tk), :]
        for n in range(2):
            ka = jnp.concatenate([k[:, n * HEAD_DIM:(n + 1) * HEAD_DIM], aug_k[:tk]], axis=1)
            s = lax.dot_general(ka, qa[n], (((1,), (1,)), ((), ())), preferred_element_type=F32)
            dst_refs[n][...] = s * scale2

    def consume(kb, src_refs, carry, mask_shift):
        vt = vt_ref[kb]
        c = slope2 * (kb * tk - qi * tq).astype(F32)
        new_carry, alphas, ps = [], [], []
        for n in range(2):
            m, l = carry[2 * n], carry[2 * n + 1]
            s = src_refs[n][...]
            if mask_shift is not None:
                key = lax.broadcasted_iota(jnp.int32, (tk, tq), 0)
                qry = lax.broadcasted_iota(jnp.int32, (tk, tq), 1)
                s = jnp.where(key + mask_shift <= qry, s, NEG_BIG)
            m_new = jnp.maximum(m, jnp.max(s, axis=0, keepdims=True) + c)
            alpha = jnp.exp2(m - m_new)
            p = jnp.exp2(s - (m_new - c))
            new_carry += [m_new, alpha * l + jnp.sum(p, axis=0, keepdims=True)]
            alphas.append(alpha)
            ps.append(p.astype(BF16))
        for n in range(2):
            acc_refs[n][...] = alphas[n] * acc_refs[n][...] + jnp.dot(
                vt, ps[n], preferred_element_type=F32)
        return tuple(new_carry)

    buf_a = (sa1_ref, sa2_ref)
    buf_b = (sb1_ref, sb2_ref)
    qk_store(0, buf_a)

    def pair(t, carry):
        kb = 2 * t
        qk_store(kb + 1, buf_b)
        carry = consume(kb, buf_a, carry, None)
        qk_store(kb + 2, buf_a)
        return consume(kb + 1, buf_b, carry, None)

    m0 = jnp.full((1, tq), NEG_BIG, F32)
    l0 = jnp.zeros((1, tq), F32)
    carry = lax.fori_loop(0, qi, pair, (m0, l0, m0, l0))
    qk_store(2 * qi + 1, buf_b)
    carry = consume(2 * qi, buf_a, carry, 0)
    m1, l1, m2, l2 = consume(2 * qi + 1, buf_b, carry, tk)

    lam = _lambda_value(lq1_ref, lk1_ref, lq2_ref, lk2_ref, lam_init)
    o_t = acc1_ref[...] / l1 - lam * (acc2_ref[...] / l2)
    ms = jnp.mean(o_t * o_t, axis=0, keepdims=True)
    y_t = o_t * lax.rsqrt(ms + RMS_EPS) * subt_ref[...] * (1.0 - lam_init)
    o_ref[...] = y_t.T.astype(BF16)


def _flash_attention(q, k, v, slopes, lams, sub_w, batch, seq, n_heads, lam_init, tq=512):
    kd = 2 * HEAD_DIM
    nq = seq // tq
    lam_spec = pl.BlockSpec((1, HEAD_DIM), lambda b, h, i: (0, 0))
    return pl.pallas_call(
        functools.partial(_flash_kernel, tq=tq, lam_init=lam_init),
        grid=(batch, n_heads, nq),
        in_specs=[
            pl.BlockSpec(memory_space=pltpu.SMEM),
            pl.BlockSpec((tq, kd), lambda b, h, i: (b * nq + i, h)),
            pl.BlockSpec((seq, kd), lambda b, h, i: (b, h)),
            pl.BlockSpec((seq, kd), lambda b, h, i: (b, h)),
            lam_spec, lam_spec, lam_spec, lam_spec,
            pl.BlockSpec((kd, 1), lambda b, h, i: (0, 0)),
        ],
        out_specs=pl.BlockSpec((tq, kd), lambda b, h, i: (b * nq + i, h)),
        out_shape=jax.ShapeDtypeStruct(q.shape, BF16),
        scratch_shapes=[pltpu.VMEM((kd, tq), F32), pltpu.VMEM((kd, tq), F32),
                        pltpu.VMEM((2 * nq, kd, tq // 2), BF16)]
                       + [pltpu.VMEM((tq // 2, tq), F32)] * 4,
        compiler_params=_cparams(("parallel", "parallel", "arbitrary")),
        name="prompt_attention",
    )(slopes, q, k, v, *lams, sub_w.reshape(kd, 1))


def _decode_kernel(pt_ref, q_ref, kn_ref, vn_ref, lq1_ref, lk1_ref, lq2_ref, lk2_ref,
                   sub_ref, slope_ref, tbl_ref, ck_ref, cv_ref,
                   o_ref, m_ref, l_ref, acc_ref, s_even_ref, s_odd_ref, kbuf_ref, vbuf_ref, sem_ref,
                   *, pages_per_step, page, n_pages, lam_init, tokens_per_iter, layer, n_slots):
    s_refs = (s_even_ref, s_odd_ref)
    g = pl.program_id(1)
    steps = pl.num_programs(1)
    step = pl.program_id(0) * steps + g
    total = pl.num_programs(0) * steps
    past = n_pages * page
    n_iter = page // tokens_per_iter

    def page_copies(t, slot):
        seq, grp = t // steps, t % steps
        out = []
        for r in range(pages_per_step):
            pid = pt_ref[seq, grp * pages_per_step + r]
            out.append(pltpu.make_async_copy(ck_ref.at[layer, pid], kbuf_ref.at[slot, r],
                                             sem_ref.at[0, slot]))
            out.append(pltpu.make_async_copy(cv_ref.at[layer, pid], vbuf_ref.at[slot, r],
                                             sem_ref.at[1, slot]))
        return out

    @pl.when(step == 0)
    def _():
        for t in range(n_slots):
            @pl.when(t < total)
            def _():
                for cp in page_copies(t, t):
                    cp.start()

    ahead = step + (n_slots - 1)

    @pl.when((step > 0) & (ahead < total))
    def _():
        for cp in page_copies(ahead, ahead % n_slots):
            cp.start()

    slot = step % n_slots
    for cp in page_copies(step, slot):
        cp.wait()
    k_refs = [kbuf_ref.at[slot, r] for r in range(pages_per_step)]
    v_refs = [vbuf_ref.at[slot, r] for r in range(pages_per_step)]

    @pl.when(g == 0)
    def _():
        m_ref[...] = jnp.full_like(m_ref, NEG_BIG)
        l_ref[...] = jnp.zeros_like(l_ref)
        acc_ref[...] = jnp.zeros_like(acc_ref)

    qv = q_ref[0] * (HEAD_DIM ** -0.5 * LOG2E)
    q1 = qv[:, :HEAD_DIM]
    q2 = qv[:, HEAD_DIM:]
    slope2 = slope_ref[...]

    def rowsum(x):
        return jnp.broadcast_to(jnp.sum(x, axis=-1, keepdims=True), x.shape)

    def twice(p):
        return jnp.concatenate([p, p], axis=-1)

    def run(score_page, acc_page, shifts):
        if score_page is not None:
            k_ref, sa_ref = k_refs[score_page], s_refs[score_page % 2]
        if acc_page is not None:
            v_ref, sb_ref = v_refs[acc_page], s_refs[acc_page % 2]

        def body(it, carry):
            mx1, mx2, parts = carry
            parts = [list(p) for p in parts]
            base = it * tokens_per_iter
            for tt in range(tokens_per_iter):
                t = base + tt
                if score_page is not None:
                    kt = k_ref[t]
                    tb = tbl_ref[t]
                    u1 = rowsum(kt[:, :HEAD_DIM] * q1) + tb
                    u2 = rowsum(kt[:, HEAD_DIM:] * q2) + tb
                    sa_ref[t, 0] = u1
                    sa_ref[t, 1] = u2
                    mx1 = jnp.maximum(mx1, u1)
                    mx2 = jnp.maximum(mx2, u2)
                if acc_page is not None:
                    p1 = jnp.exp2(sb_ref[t, 0] - shifts[0])
                    p2 = jnp.exp2(sb_ref[t, 1] - shifts[1])
                    vt = v_ref[t]
                    part = parts[tt % PARTIAL_SUMS]
                    part[0] = part[0] + p1
                    part[1] = part[1] + p2
                    part[2] = part[2] + twice(p1) * vt
                    part[3] = part[3] + twice(p2) * vt
            return mx1, mx2, tuple(tuple(p) for p in parts)

        neg = jnp.full((SUBLANES, LANES), NEG_BIG, F32)
        zero = (jnp.zeros_like(l_ref[0]), jnp.zeros_like(l_ref[0]),
                jnp.zeros_like(acc_ref[0]), jnp.zeros_like(acc_ref[0]))
        first = (l_ref[0], l_ref[1], acc_ref[0], acc_ref[1])
        if acc_page is None:
            parts0 = ()
        else:
            parts0 = (first,) + (zero,) * (PARTIAL_SUMS - 1)
        mx1, mx2, parts = lax.fori_loop(0, n_iter, body, (neg, neg, parts0))
        if acc_page is not None:
            l_ref[0] = sum(p[0] for p in parts[1:]) + parts[0][0]
            l_ref[1] = sum(p[1] for p in parts[1:]) + parts[0][1]
            acc_ref[0] = sum(p[2] for p in parts[1:]) + parts[0][2]
            acc_ref[1] = sum(p[3] for p in parts[1:]) + parts[0][3]
        return mx1, mx2

    def new_max(c, m_cand):
        m_new = jnp.maximum(m_ref[c], m_cand)
        alpha = jnp.exp2(m_ref[c] - m_new)
        m_ref[c] = m_new
        l_ref[c] = alpha * l_ref[c]
        acc_ref[c] = twice(alpha) * acc_ref[c]
        return m_new

    def page_shifts(r, mx1, mx2):
        first_pos = (g * pages_per_step + r) * page
        c = slope2 * (first_pos - past).astype(F32)
        return new_max(0, mx1 + c) - c, new_max(1, mx2 + c) - c

    mx = run(0, None, None)
    for r in range(pages_per_step):
        shifts = page_shifts(r, *mx)
        if r + 1 < pages_per_step:
            mx = run(r + 1, r, shifts)
        else:
            run(None, r, shifts)

    @pl.when(g == pl.num_programs(1) - 1)
    def _():
        kn = kn_ref[0]
        vn = vn_ref[0]
        s1 = rowsum(kn[:, :HEAD_DIM] * q1)
        s2 = rowsum(kn[:, HEAD_DIM:] * q2)
        p1 = jnp.exp2(s1 - new_max(0, s1))
        p2 = jnp.exp2(s2 - new_max(1, s2))
        o1 = (acc_ref[0] + twice(p1) * vn) / twice(l_ref[0] + p1)
        o2 = (acc_ref[1] + twice(p2) * vn) / twice(l_ref[1] + p2)
        lam = _lambda_value(lq1_ref, lk1_ref, lq2_ref, lk2_ref, lam_init)
        o_ref[0] = _head_out(o1 - lam * o2, sub_ref[...], lam_init)


def _decode_attention(q, k_new, v_new, cache_k, cache_v, page_table, layer, slopes, lams,
                      sub_w, lam_init, pages_per_step=8, tokens_per_iter=128, n_slots=3):
    nb, n_heads, kd = q.shape
    n_pages = page_table.shape[1]
    page = cache_k.shape[2]
    steps = n_pages // pages_per_step
    slope2 = jnp.broadcast_to((slopes * LOG2E)[:, None], (n_heads, LANES))
    tok_bias = jnp.arange(page, dtype=F32)[:, None, None] * slope2[None]

    tok_spec = pl.BlockSpec((1, n_heads, kd), lambda b, g, pt: (b, 0, 0))
    lam_spec = pl.BlockSpec((1, HEAD_DIM), lambda b, g, pt: (0, 0))

    any_spec = pl.BlockSpec(memory_space=pl.ANY)
    ring = (n_slots, pages_per_step, page, n_heads, kd)
    grid_spec = pltpu.PrefetchScalarGridSpec(
        num_scalar_prefetch=1,
        grid=(nb, steps),
        in_specs=[tok_spec, tok_spec, tok_spec, lam_spec, lam_spec, lam_spec, lam_spec,
                  pl.BlockSpec((1, kd), lambda b, g, pt: (0, 0)),
                  pl.BlockSpec((n_heads, LANES), lambda b, g, pt: (0, 0)),
                  pl.BlockSpec((page, n_heads, LANES), lambda b, g, pt: (0, 0, 0)),
                  any_spec, any_spec],
        out_specs=tok_spec,
        scratch_shapes=[
            pltpu.VMEM((2, n_heads, LANES), F32),
            pltpu.VMEM((2, n_heads, LANES), F32),
            pltpu.VMEM((2, n_heads, kd), F32),
            pltpu.VMEM((page, 2, n_heads, LANES), F32),
            pltpu.VMEM((page, 2, n_heads, LANES), F32),
            pltpu.VMEM(ring, F32),
            pltpu.VMEM(ring, F32),
            pltpu.SemaphoreType.DMA((2, n_slots)),
        ],
    )
    return pl.pallas_call(
        functools.partial(_decode_kernel, pages_per_step=pages_per_step, page=page,
                          n_pages=n_pages, lam_init=lam_init, tokens_per_iter=tokens_per_iter,
                          layer=layer, n_slots=n_slots),
        grid_spec=grid_spec,
        out_shape=jax.ShapeDtypeStruct((nb, n_heads, kd), F32),
        compiler_params=_cparams(("arbitrary", "arbitrary")),
        name="sample_attention",
    )(page_table, q, k_new, v_new, *lams, sub_w, slope2, tok_bias, cache_k, cache_v)


def _merge_kernel(z_ref, halo_ref, bg_ref, o_ref, ga_ref, gb_ref, x_ref, cw_ref, wc_ref, wa_ref,
                  wo_ref, pw_ref, fw_ref, out_ref, h_ref, zbuf_ref, *, tm, tiles_per_seq):
    i = pl.program_id(0)
    z = z_ref[...]
    starts_seq = i % tiles_per_seq == 0
    zbuf_ref[0:SUBLANES, :] = jnp.where(starts_seq, 0.0, halo_ref[...])
    zbuf_ref[SUBLANES:, :] = z
    z1 = zbuf_ref[SUBLANES - 1:SUBLANES - 1 + tm, :]
    z2 = zbuf_ref[SUBLANES - 2:SUBLANES - 2 + tm, :]
    cw = cw_ref[...]
    zc = z2 * cw[0:1] + z1 * cw[1:2] + z * cw[2:3]
    y_conv = jnp.dot((bg_ref[...] * zc).astype(BF16), wc_ref[...], preferred_element_type=F32)
    y_attn = jnp.dot(o_ref[...], wa_ref[...], preferred_element_type=F32)
    merged = jax.nn.sigmoid(ga_ref[...]) * y_conv + jax.nn.sigmoid(gb_ref[...]) * y_attn
    mo = jnp.dot(merged.astype(BF16), wo_ref[...], preferred_element_type=F32)
    x_new = x_ref[...] + _rms(mo, pw_ref[...])
    out_ref[...] = x_new
    h_ref[...] = _rms(x_new, fw_ref[...]).astype(BF16)


def _merge(z, bg, o, gates, x, conv_w, wc, wa, wo, post_w, ffn_w, tm, seq, name):
    rows, d = x.shape
    dc = z.shape[1]
    row = lambda w: pl.BlockSpec((tm, w), lambda i: (i, 0))
    const = lambda a: pl.BlockSpec(a.shape, lambda i: (0, 0), pipeline_mode=pl.Buffered(1))
    hb = tm // SUBLANES
    halo = pl.BlockSpec((SUBLANES, dc), lambda i: (jnp.maximum(i * hb - 1, 0), 0))
    return pl.pallas_call(
        functools.partial(_merge_kernel, tm=tm, tiles_per_seq=seq // tm),
        grid=(rows // tm,),
        in_specs=[row(dc), halo, row(dc), row(d),
                  pl.BlockSpec((tm, d), lambda i: (i, 0)),
                  pl.BlockSpec((tm, d), lambda i: (i, 1)),
                  row(d), const(conv_w), const(wc), const(wa), const(wo), const(post_w),
                  const(ffn_w)],
        out_specs=[row(d), row(d)],
        out_shape=[jax.ShapeDtypeStruct((rows, d), F32), jax.ShapeDtypeStruct((rows, d), BF16)],
        scratch_shapes=[pltpu.VMEM((tm + SUBLANES, dc), F32)],
        compiler_params=_cparams(("arbitrary",)),
        name=name,
    )(z, z, bg, o, gates, gates, x, conv_w, wc, wa, wo, post_w, ffn_w)


def _ffn_kernel(x_ref, h_ref, wg_ref, wv_ref, cwg_ref, cwv_ref, wd_ref, pw_ref,
                out_ref, sg_ref, sv_ref, acc_ref, cg_ref, cv_ref, ug_ref, uv_ref,
                *, tm, tiles_per_seq):
    i = pl.program_id(0)
    j = pl.program_id(1)

    @pl.when(j == 0)
    def _():
        acc_ref[...] = jnp.zeros_like(acc_ref)

        @pl.when(i % tiles_per_seq == 0)
        def _():
            cg_ref[...] = jnp.zeros_like(cg_ref)
            cv_ref[...] = jnp.zeros_like(cv_ref)

    h = h_ref[...]
    tf = wd_ref.shape[0]

    def up_proj(cs, w_ref, carry_ref, ubuf_ref, state_ref):
        up = jnp.dot(h, w_ref[:, cs], preferred_element_type=F32)
        ubuf_ref[0:SUBLANES, cs] = carry_ref[j, :, cs]
        ubuf_ref[SUBLANES:, cs] = up
        carry_ref[j, :, cs] = up[tm - SUBLANES:, :]
        state_ref[:, cs] = up[tm - (CONV_W - 1):, :]

    def conv(cs, cw_ref, ubuf_ref):
        cw = cw_ref[:, cs]
        u0 = ubuf_ref[SUBLANES:SUBLANES + tm, cs]
        u1 = ubuf_ref[SUBLANES - 1:SUBLANES - 1 + tm, cs]
        u2 = ubuf_ref[SUBLANES - 2:SUBLANES - 2 + tm, cs]
        return u2 * cw[0:1] + u1 * cw[1:2] + u0 * cw[2:3]

    chunks = [slice(c0, c0 + FFN_SUB) for c0 in range(0, tf, FFN_SUB)]
    for cs in chunks:
        up_proj(cs, wg_ref, cg_ref, ug_ref, sg_ref)
        up_proj(cs, wv_ref, cv_ref, uv_ref, sv_ref)
    down = None
    for cs in chunks:
        gate = conv(cs, cwg_ref, ug_ref)
        val = conv(cs, cwv_ref, uv_ref)
        act = (gate * jax.nn.sigmoid(gate)) * val
        part = jnp.dot(act.astype(BF16), wd_ref[cs, :], preferred_element_type=F32)
        down = part if down is None else down + part
    acc_ref[...] += down

    @pl.when(j == pl.num_programs(1) - 1)
    def _():
        out_ref[...] = x_ref[...] + _rms(acc_ref[...], pw_ref[...])


def _ffn(x, h, wg, wv, conv_w, w_down, post_w, tm, seq, tf, name):
    rows, d = x.shape
    d_ff = w_down.shape[0]
    nj = d_ff // tf
    st_spec = pl.BlockSpec((None, CONV_W - 1, tf), lambda i, j: (i, 0, j))
    st_shape = jax.ShapeDtypeStruct((rows // tm, CONV_W - 1, d_ff), F32)
    return pl.pallas_call(
        functools.partial(_ffn_kernel, tm=tm, tiles_per_seq=seq // tm),
        grid=(rows // tm, nj),
        in_specs=[
            pl.BlockSpec((tm, d), lambda i, j: (i, 0)),
            pl.BlockSpec((tm, d), lambda i, j: (i, 0)),
            pl.BlockSpec((d, tf), lambda i, j: (0, j)),
            pl.BlockSpec((d, tf), lambda i, j: (0, j)),
            pl.BlockSpec((CONV_W, tf), lambda i, j: (0, j)),
            pl.BlockSpec((CONV_W, tf), lambda i, j: (0, nj + j)),
            pl.BlockSpec((tf, d), lambda i, j: (j, 0)),
            pl.BlockSpec((1, d), lambda i, j: (0, 0)),
        ],
        out_specs=[pl.BlockSpec((tm, d), lambda i, j: (i, 0)), st_spec, st_spec],
        out_shape=[jax.ShapeDtypeStruct((rows, d), F32), st_shape, st_shape],
        scratch_shapes=[
            pltpu.VMEM((tm, d), F32),
            pltpu.VMEM((nj, SUBLANES, tf), F32), pltpu.VMEM((nj, SUBLANES, tf), F32),
            pltpu.VMEM((tm + SUBLANES, tf), F32), pltpu.VMEM((tm + SUBLANES, tf), F32)],
        compiler_params=_cparams(("arbitrary", "arbitrary")),
        name=name,
    )(x, h, wg, wv, conv_w, conv_w, w_down, post_w)


def _s_proj_kernel(h_ref, w_ref, y_ref, wb_ref):
    wb = w_ref[...].astype(BF16)
    wb_ref[...] = wb
    y_ref[...] = jnp.dot(h_ref[...], wb, preferred_element_type=F32)


def _s_proj(h, w, col0, ncols, tn, name):
    nb, d = h.shape
    c0 = col0 // tn
    return pl.pallas_call(
        _s_proj_kernel,
        grid=(ncols // tn,),
        in_specs=[pl.BlockSpec((nb, d), lambda j: (0, 0)),
                  pl.BlockSpec((d, tn), lambda j: (0, c0 + j))],
        out_specs=[pl.BlockSpec((nb, tn), lambda j: (0, j)), pl.BlockSpec((d, tn), lambda j: (0, j))],
        out_shape=[jax.ShapeDtypeStruct((nb, ncols), F32), jax.ShapeDtypeStruct((d, ncols), BF16)],
        compiler_params=_cparams(("parallel",)),
        name=name,
    )(h, w)


def _s_conv_proj_kernel(h_ref, wu_ref, wb_ref, wc_ref, z_ref, bg_ref, wu_o, wb_o, wc_o):
    h = h_ref[...]
    wu = wu_ref[...].astype(BF16)
    wb = wb_ref[...].astype(BF16)
    wc = wc_ref[...].astype(BF16)
    wu_o[...] = wu
    wb_o[...] = wb
    wc_o[...] = wc
    u = jnp.dot(h, wu, preferred_element_type=F32)
    cg = jnp.dot(h, wc, preferred_element_type=F32)
    z_ref[...] = cg * u
    bg_ref[...] = jnp.dot(h, wb, preferred_element_type=F32)


def _s_conv_proj(h, w, d_conv, tn, name):
    nb, d = h.shape
    nt = d_conv // tn
    y_spec = pl.BlockSpec((nb, tn), lambda j: (0, j))
    w_out = pl.BlockSpec((d, tn), lambda j: (0, j))
    return pl.pallas_call(
        _s_conv_proj_kernel,
        grid=(nt,),
        in_specs=[pl.BlockSpec((nb, d), lambda j: (0, 0)),
                  pl.BlockSpec((d, tn), lambda j: (0, j)),
                  pl.BlockSpec((d, tn), lambda j: (0, nt + j)),
                  pl.BlockSpec((d, tn), lambda j: (0, 2 * nt + j))],
        out_specs=[y_spec, y_spec, w_out, w_out, w_out],
        out_shape=[jax.ShapeDtypeStruct((nb, d_conv), F32)] * 2
                  + [jax.ShapeDtypeStruct((d, d_conv), BF16)] * 3,
        compiler_params=_cparams(("parallel",)),
        name=name,
    )(h, w, w, w)


def _s_mix_kernel(z_ref, st0_ref, st1_ref, bg_ref, o_ref, ga_ref, gb_ref, cw_ref, wc_ref, wa_ref,
                  m_ref, wc_o, wa_o):
    cw = cw_ref[...]
    zc = st0_ref[...] * cw[0:1] + st1_ref[...] * cw[1:2] + z_ref[...] * cw[2:3]
    wc = wc_ref[...].astype(BF16)
    wa = wa_ref[...].astype(BF16)
    wc_o[...] = wc
    wa_o[...] = wa
    y_conv = jnp.dot((bg_ref[...] * zc).astype(BF16), wc, preferred_element_type=F32)
    y_attn = jnp.dot(o_ref[...], wa, preferred_element_type=F32)
    merged = jax.nn.sigmoid(ga_ref[...]) * y_conv + jax.nn.sigmoid(gb_ref[...]) * y_attn
    m_ref[...] = merged.astype(BF16)


def _s_mix(z, st0, st1, bg, o, gates, conv_w, w_conv_out, w_attn_out, tn, name):
    nb, dc = z.shape
    da, d = w_attn_out.shape
    nt = d // tn
    full = lambda a: pl.BlockSpec(a.shape, lambda j: (0, 0))
    return pl.pallas_call(
        _s_mix_kernel,
        grid=(nt,),
        in_specs=[full(z), full(st0), full(st1), full(bg), full(o),
                  pl.BlockSpec((nb, tn), lambda j: (0, j)),
                  pl.BlockSpec((nb, tn), lambda j: (0, nt + j)),
                  full(conv_w),
                  pl.BlockSpec((dc, tn), lambda j: (0, j)),
                  pl.BlockSpec((da, tn), lambda j: (0, j))],
        out_specs=[pl.BlockSpec((nb, tn), lambda j: (0, j)),
                   pl.BlockSpec((dc, tn), lambda j: (0, j)),
                   pl.BlockSpec((da, tn), lambda j: (0, j))],
        out_shape=[jax.ShapeDtypeStruct((nb, d), BF16),
                   jax.ShapeDtypeStruct(w_conv_out.shape, BF16),
                   jax.ShapeDtypeStruct(w_attn_out.shape, BF16)],
        compiler_params=_cparams(("parallel",)),
        name=name,
    )(z, st0, st1, bg, o, gates, gates, conv_w, w_conv_out, w_attn_out)


def _s_out_kernel(m_ref, wo_ref, x_ref, pw_ref, fw_ref, out_ref, h_ref, wo_o, acc_ref):
    k = pl.program_id(0)

    @pl.when(k == 0)
    def _():
        acc_ref[...] = jnp.zeros_like(acc_ref)

    wo = wo_ref[...].astype(BF16)
    wo_o[...] = wo
    acc_ref[...] += jnp.dot(m_ref[...], wo, preferred_element_type=F32)

    @pl.when(k == pl.num_programs(0) - 1)
    def _():
        x_new = x_ref[...] + _rms(acc_ref[...], pw_ref[...])
        out_ref[...] = x_new
        h_ref[...] = _rms(x_new, fw_ref[...]).astype(BF16)


def _s_out(merged, w_o, x, post_w, ffn_w, tk, name):
    nb, d = x.shape
    full = lambda a: pl.BlockSpec(a.shape, lambda k: (0, 0))
    return pl.pallas_call(
        _s_out_kernel,
        grid=(d // tk,),
        in_specs=[pl.BlockSpec((nb, tk), lambda k: (0, k)),
                  pl.BlockSpec((tk, d), lambda k: (k, 0)),
                  full(x), full(post_w), full(ffn_w)],
        out_specs=[full(x), full(x), pl.BlockSpec((tk, d), lambda k: (k, 0))],
        out_shape=[jax.ShapeDtypeStruct((nb, d), F32), jax.ShapeDtypeStruct((nb, d), BF16),
                   jax.ShapeDtypeStruct(w_o.shape, BF16)],
        scratch_shapes=[pltpu.VMEM((nb, d), F32)],
        compiler_params=_cparams(("arbitrary",)),
        name=name,
    )(merged, w_o, x, post_w, ffn_w)


def _s_ffn_kernel(x_ref, h_ref, wg_ref, wv_ref, cwg_ref, cwv_ref, wd_ref, pw_ref,
                  g0_ref, g1_ref, v0_ref, v1_ref,
                  out_ref, ug_ref, uv_ref, wg_o, wv_o, wd_o, acc_ref):
    j = pl.program_id(0)

    @pl.when(j == 0)
    def _():
        acc_ref[...] = jnp.zeros_like(acc_ref)

    h = h_ref[...]
    wg = wg_ref[...].astype(BF16)
    wv = wv_ref[...].astype(BF16)
    wd = wd_ref[...].astype(BF16)
    wg_o[...] = wg
    wv_o[...] = wv
    wd_o[...] = wd
    up_g = jnp.dot(h, wg, preferred_element_type=F32)
    up_v = jnp.dot(h, wv, preferred_element_type=F32)
    ug_ref[...] = up_g
    uv_ref[...] = up_v
    cwg = cwg_ref[...]
    cwv = cwv_ref[...]
    gate = g0_ref[...] * cwg[0:1] + g1_ref[...] * cwg[1:2] + up_g * cwg[2:3]
    val = v0_ref[...] * cwv[0:1] + v1_ref[...] * cwv[1:2] + up_v * cwv[2:3]
    act = (gate * jax.nn.sigmoid(gate)) * val
    acc_ref[...] += jnp.dot(act.astype(BF16), wd, preferred_element_type=F32)

    @pl.when(j == pl.num_programs(0) - 1)
    def _():
        out_ref[...] = x_ref[...] + _rms(acc_ref[...], pw_ref[...])


def _s_ffn(x, h, w_up, conv_w, w_down, post_w, prev, tf, name):
    nb, d = x.shape
    d_ff = w_down.shape[0]
    nj = d_ff // tf
    full = lambda a: pl.BlockSpec(a.shape, lambda j: (0, 0))
    col = pl.BlockSpec((nb, tf), lambda j: (0, j))
    up_w = pl.BlockSpec((d, tf), lambda j: (0, j))
    down_w = pl.BlockSpec((tf, d), lambda j: (j, 0))
    return pl.pallas_call(
        _s_ffn_kernel,
        grid=(nj,),
        in_specs=[full(x), full(h), up_w, pl.BlockSpec((d, tf), lambda j: (0, nj + j)),
                  pl.BlockSpec((CONV_W, tf), lambda j: (0, j)),
                  pl.BlockSpec((CONV_W, tf), lambda j: (0, nj + j)),
                  down_w, full(post_w), col, col, col, col],
        out_specs=[full(x), col, col, up_w, up_w, down_w],
        out_shape=[jax.ShapeDtypeStruct((nb, d), F32),
                   jax.ShapeDtypeStruct((nb, d_ff), F32), jax.ShapeDtypeStruct((nb, d_ff), F32),
                   jax.ShapeDtypeStruct((d, d_ff), BF16), jax.ShapeDtypeStruct((d, d_ff), BF16),
                   jax.ShapeDtypeStruct((d_ff, d), BF16)],
        scratch_shapes=[pltpu.VMEM((nb, d), F32)],
        compiler_params=_cparams(("arbitrary",)),
        name=name,
    )(x, h, w_up, w_up, conv_w, conv_w, w_down, post_w, *prev)


def kernel(x_prompt, x_sample, cache_k, cache_v, page_table, state_conv, state_ffn_conv,
           pre_mix_w, w_in, conv_w, w_conv_out, lambda_q1, lambda_k1, lambda_q2, lambda_k2,
           subln_w, w_attn_out, w_o, post_mix_w, pre_ffn_w, w_up, ffn_conv_w, w_down, post_ffn_w):
    batch, seq, d = x_prompt.shape
    nb = x_sample.shape[0]
    depth = w_in.shape[0]
    n_heads = cache_k.shape[3]
    kd = cache_k.shape[4]
    d_conv = conv_w.shape[2]
    d_ff = w_down.shape[1]
    rows = batch * seq
    qkv = n_heads * kd
    c_q, c_k, c_v, c_g = 3 * d_conv, 3 * d_conv + qkv, 3 * d_conv + 2 * qkv, 3 * d_conv + 3 * qkv
    slopes = jnp.exp2(-8.0 * jnp.arange(1, n_heads + 1, dtype=F32) / n_heads)

    xp = x_prompt.reshape(rows, d)
    xs = x_sample.reshape(nb, d)
    outs = {n: [] for n in ("kp", "vp", "cp", "fp", "ks", "vs", "cs", "fs")}
    tm, tn = 1024, 1024
    tm_merge, tm_ffn, tf = 256, 512, 512
    ts = 512

    for l in range(depth):
        lam_init = 0.8 - 0.6 * math.exp(-0.3 * l)
        row2 = lambda a: a[l].reshape(1, -1)
        lams = [row2(lambda_q1), row2(lambda_k1), row2(lambda_q2), row2(lambda_k2)]
        pre_w, post_w = row2(pre_mix_w), row2(post_mix_w)
        pre_f, post_f, sub_w = row2(pre_ffn_w), row2(post_ffn_w), row2(subln_w)

        hs = _norm(xs, pre_w, nb)
        zs, bgs, wu_b, wb_b, wcg_b = _s_conv_proj(hs, w_in[l], d_conv, ts, "s_conv_proj")
        qkv_s, wqkv_b = _s_proj(hs, w_in[l], c_q, 3 * qkv, ts, "s_qkv_proj")
        gates_s, wg_b = _s_proj(hs, w_in[l], c_g, 2 * d, ts, "s_gate_proj")
        q_s = qkv_s[:, :qkv].reshape(nb, n_heads, kd)
        k_s = qkv_s[:, qkv:2 * qkv].reshape(nb, n_heads, kd)
        v_s = qkv_s[:, 2 * qkv:].reshape(nb, n_heads, kd)
        o_s = _decode_attention(q_s, k_s, v_s, cache_k, cache_v, page_table, l, slopes, lams,
                                sub_w, lam_init)
        sc = state_conv[l]
        merged_s, wc_b, wa_b = _s_mix(zs, sc[:, 0], sc[:, 1], bgs, o_s.reshape(nb, qkv).astype(BF16),
                                      gates_s, conv_w[l], w_conv_out[l], w_attn_out[l], ts, "s_mix")
        xs, hfs, wo_b = _s_out(merged_s, w_o[l], xs, post_w, pre_f, ts, "s_out")
        sf = state_ffn_conv[l]
        prev = (sf[:, 0, :d_ff], sf[:, 1, :d_ff], sf[:, 0, d_ff:], sf[:, 1, d_ff:])
        xs, ug, uv, wug_b, wuv_b, wd_b = _s_ffn(xs, hfs, w_up[l], ffn_conv_w[l], w_down[l], post_f,
                                                prev, tf, "s_ffn")
        outs["ks"].append(k_s.reshape(nb, 1, n_heads, kd))
        outs["vs"].append(v_s.reshape(nb, 1, n_heads, kd))
        outs["cs"].append(jnp.stack([sc[:, 1], zs], axis=1))
        outs["fs"].append(jnp.stack([sf[:, 1], jnp.concatenate([ug, uv], axis=-1)], axis=1))

        hp = _norm(xp, pre_w, 512)
        z, bg = _conv_proj(hp, wu_b, wb_b, wcg_b, tm, 512, "p_conv_proj")
        q_b = _proj(hp, wqkv_b, 0, qkv, BF16, tm, tn, "p_q_proj")
        k_f, k_b = _kv_proj(hp, wqkv_b, 1, n_heads, kd, 512, "p_k_proj")
        v_f, v_b = _kv_proj(hp, wqkv_b, 2, n_heads, kd, 512, "p_v_proj")
        gates = _proj(hp, wg_b, 0, 2 * d, F32, tm, tn, "p_gate_proj")
        o = _flash_attention(q_b, k_b, v_b, slopes, lams, sub_w, batch, seq, n_heads, lam_init)
        xp, hf = _merge(z, bg, o, gates, xp, conv_w[l], wc_b, wa_b, wo_b, post_w, pre_f,
                        tm_merge, seq, "p_merge")
        xp, fg, fv = _ffn(xp, hf, wug_b, wuv_b, ffn_conv_w[l], wd_b, post_f, tm_ffn, seq, tf, "p_ffn")
        last = seq // tm_ffn - 1
        outs["kp"].append(k_f.reshape(batch, seq, n_heads, kd))
        outs["vp"].append(v_f.reshape(batch, seq, n_heads, kd))
        outs["cp"].append(z.reshape(batch, seq, d_conv)[:, seq - (CONV_W - 1):])
        outs["fp"].append(jnp.concatenate([fg, fv], axis=-1)[last::seq // tm_ffn])

    st = lambda n: jnp.stack(outs[n])
    return (xp.reshape(batch, seq, d), xs.reshape(nb, 1, d),
            st("kp"), st("vp"), st("cp"), st("fp"),
            st("ks"), st("vs"), st("cs"), st("fs"))
```
